```python
import math
import jax, jax.numpy as jnp
from jax import lax
import numpy as np

D_MODEL = 1024
BATCH = 32
SEQ = 2048
DEPTH = 4
DEC_BATCH = 8
DEC_SEQ = 64
PAST_LEN = 1024

CHUNK = 64
Q_BLOCK = 128
H_A = 8
HD_A = 64
W_A = H_A * HD_A
H_B = 8
HD_B = 64
W_B = H_B * HD_B
H_IDX = 8
D_IDX = 64
MAX_SELECT = 256
N_BUCKETS = 32
MAX_DISTANCE = 128
LN_EPS = 1e-5
ALPHA = (2 * DEPTH) ** 0.25
BETA = (8 * DEPTH) ** -0.25
SB_SCALE = HD_A ** -0.5
ATT_SCALE = HD_B ** -0.5
SPLIT_SIZES = (W_A, W_A, W_A, W_A, W_B, HD_B, HD_B, W_B, H_IDX * D_IDX, D_IDX, H_IDX, D_MODEL, D_MODEL)
D_IN = 4 * W_A + 2 * W_B + 2 * HD_B + H_IDX * D_IDX + D_IDX + H_IDX + 2 * D_MODEL

kernel_name = 'stickbreak_dsa_hybrid_stream_step'


def layer_norm(x, g, b):
    xf = x.astype(jnp.float32)
    mu = jnp.mean(xf, axis=-1, keepdims=True)
    var = jnp.mean(jnp.square(xf - mu), axis=-1, keepdims=True)
    return ((xf - mu) * lax.rsqrt(var + LN_EPS) * g + b).astype(x.dtype)


def t5_bucket(rel):
    half = N_BUCKETS // 2
    max_exact = half // 2
    n = jnp.abs(rel)
    n_f = jnp.maximum(n, 1).astype(jnp.float32)
    large = max_exact + (jnp.log(n_f / max_exact) / math.log(MAX_DISTANCE / max_exact)
                         * (half - max_exact)).astype(jnp.int32)
    large = jnp.minimum(large, half - 1)
    return jnp.where(rel > 0, half, 0) + jnp.where(n < max_exact, n, large)


def stick_breaking(q, k, v, q_pos, k_pos):
    z = jnp.einsum('bthd,bshd->bhts', q, k).astype(jnp.float32) * SB_SCALE
    before = k_pos[None, :] < q_pos[:, None]
    log_keep = jnp.where(before, jax.nn.log_sigmoid(-z), 0.0)
    later = lax.cumsum(log_keep, axis=3, reverse=True) - log_keep
    a = jnp.where(before, jnp.exp(jax.nn.log_sigmoid(z) + later), 0.0)
    return jnp.einsum('bhts,bshd->bthd', a.astype(v.dtype), v)


def sparse_attention(q, k, v, q_idx, w_idx, k_idx, q_pos, k_pos, rel_bias, n_select):
    admissible = (k_pos[None, :] // CHUNK) <= (q_pos[:, None] // CHUNK)
    head_scores = jax.nn.relu(jnp.einsum('bthi,bsi->bths', q_idx, k_idx).astype(jnp.float32))
    index_score = jnp.einsum('bth,bths->bts', w_idx.astype(jnp.float32), head_scores)
    index_score = jnp.where(admissible[None], index_score, -jnp.inf)
    _, sel = lax.top_k(index_score, n_select)
    sel_pos = k_pos[sel]
    valid = (sel_pos // CHUNK) <= (q_pos[None, :, None] // CHUNK)
    gather = jax.vmap(lambda rows, idx: rows[idx])
    k_sel = gather(k, sel)
    v_sel = gather(v, sel)
    logits = jnp.einsum('bthd,btkd->bhtk', q, k_sel).astype(jnp.float32) * ATT_SCALE
    bias = rel_bias[t5_bucket(sel_pos - q_pos[None, :, None])]
    logits = logits + jnp.moveaxis(bias, 3, 1).astype(jnp.float32)
    logits = jnp.where(valid[:, None], logits, -jnp.inf)
    p = jax.nn.softmax(logits, axis=-1)
    return jnp.einsum('bhtk,btkd->bthd', p.astype(v.dtype), v_sel)


def _to_blocks(a):
    b, t = a.shape[:2]
    return jnp.moveaxis(a.reshape(b, t // Q_BLOCK, Q_BLOCK, *a.shape[2:]), 1, 0)


def _from_blocks(a):
    a = jnp.moveaxis(a, 0, 1)
    return a.reshape(a.shape[0], a.shape[1] * a.shape[2], *a.shape[3:])


def _blockwise(fn, q_arrays, q_pos):
    xs = tuple(_to_blocks(a) for a in q_arrays) + (q_pos.reshape(-1, Q_BLOCK),)
    return _from_blocks(lax.map(lambda args: fn(*args), xs))


def _split_points():
    return [int(c) for c in np.cumsum(SPLIT_SIZES)[:-1]]


def trunk_layer(x, past, q_pos, k_pos, blocked, n_select, w_in, b_in, w_pa, w_pb, w_out, ln_g, ln_b, rel_bias):
    b, t, _ = x.shape
    proj = x @ w_in + b_in
    (q_a, k_a, v_a, g_a, q_b, k_b, v_b, g_b, q_i, k_i, w_i, r_a, r_b) = jnp.split(proj, _split_points(), axis=-1)
    q_a = q_a.reshape(b, t, H_A, HD_A)
    k_a = k_a.reshape(b, t, H_A, HD_A)
    v_a = v_a.reshape(b, t, H_A, HD_A)
    q_b = q_b.reshape(b, t, H_B, HD_B)
    q_i = q_i.reshape(b, t, H_IDX, D_IDX)
    new_rows = (k_a, v_a, k_b, v_b, k_i)
    if past is not None:
        k_a, v_a, k_b, v_b, k_i = (jnp.concatenate([p, n], axis=1) for p, n in zip(past, new_rows))
    if blocked:
        y_a = _blockwise(lambda qa, qp: stick_breaking(qa, k_a, v_a, qp, k_pos), (q_a,), q_pos)
        y_b = _blockwise(lambda qb, qi, wi, qp: sparse_attention(qb, k_b, v_b, qi, wi, k_i, qp, k_pos, rel_bias, n_select),
                         (q_b, q_i, w_i), q_pos)
    else:
        y_a = stick_breaking(q_a, k_a, v_a, q_pos, k_pos)
        y_b = sparse_attention(q_b, k_b, v_b, q_i, w_i, k_i, q_pos, k_pos, rel_bias, n_select)
    y_a = y_a.reshape(b, t, W_A) * jax.nn.silu(g_a)
    y_b = y_b.reshape(b, t, W_B) * jax.nn.silu(g_b)
    merged = jax.nn.sigmoid(r_a) * (y_a @ w_pa) + jax.nn.sigmoid(r_b) * (y_b @ w_pb)
    x = layer_norm(ALPHA * x + merged @ w_out, ln_g, ln_b)
    return x, new_rows


def setup_inputs(seed: int = 0) -> dict:
    key = jax.random.key(seed)
    ks = jax.random.split(key, 18)
    nrm = jax.random.normal
    f32 = jnp.float32
    col_scale = jnp.concatenate([jnp.full((n,), s, f32) for n, s in zip(
        SPLIT_SIZES, (1.0, 1.0, BETA, 1.0, 1.0, 1.0, BETA, 1.0, 1.0, 1.0, 1.0, 1.0, 1.0))])
    return {
        'x_prompt': nrm(ks[0], (BATCH, SEQ, D_MODEL), f32),
        'x_sample': nrm(ks[1], (DEC_BATCH, DEC_SEQ, D_MODEL), f32),
        'cache_sb_k': nrm(ks[2], (DEPTH, DEC_BATCH, PAST_LEN, H_A, HD_A), f32),
        'cache_sb_v': BETA * nrm(ks[3], (DEPTH, DEC_BATCH, PAST_LEN, H_A, HD_A), f32),
        'cache_dsa_k': nrm(ks[4], (DEPTH, DEC_BATCH, PAST_LEN, HD_B), f32),
        'cache_dsa_v': BETA * nrm(ks[5], (DEPTH, DEC_BATCH, PAST_LEN, HD_B), f32),
        'cache_idx_k': nrm(ks[6], (DEPTH, DEC_BATCH, PAST_LEN, D_IDX), f32),
        'ln_in_g': 1.0 + 0.02 * nrm(ks[7], (D_MODEL,), f32),
        'ln_in_b': 0.02 * nrm(ks[8], (D_MODEL,), f32),
        'w_in': nrm(ks[9], (DEPTH, D_MODEL, D_IN), f32) * (D_MODEL ** -0.5) * col_scale,
        'b_in': 0.02 * nrm(ks[10], (DEPTH, D_IN), f32),
        'w_proj_a': nrm(ks[11], (DEPTH, W_A, D_MODEL), f32) * (W_A ** -0.5),
        'w_proj_b': nrm(ks[12], (DEPTH, W_B, D_MODEL), f32) * (W_B ** -0.5),
        'w_out': nrm(ks[13], (DEPTH, D_MODEL, D_MODEL), f32) * (D_MODEL ** -0.5) * BETA,
        'ln_g': 1.0 + 0.02 * nrm(ks[14], (DEPTH, D_MODEL), f32),
        'ln_b': 0.02 * nrm(ks[15], (DEPTH, D_MODEL), f32),
        'rel_bias': 0.2 * nrm(ks[16], (N_BUCKETS, H_B), f32),
    }


def reference(x_prompt, x_sample, cache_sb_k, cache_sb_v, cache_dsa_k, cache_dsa_v, cache_idx_k,
              ln_in_g, ln_in_b, w_in, b_in, w_proj_a, w_proj_b, w_out, ln_g, ln_b, rel_bias):
    seq = x_prompt.shape[1]
    dec_seq = x_sample.shape[1]
    past_len = cache_sb_k.shape[2]
    pos_prompt = jnp.arange(seq, dtype=jnp.int32)
    pos_keys_s = jnp.arange(past_len + dec_seq, dtype=jnp.int32)
    pos_query_s = pos_keys_s[past_len:]
    n_sel_prompt = min(MAX_SELECT, seq // 4)
    n_sel_sample = min(MAX_SELECT, (past_len + dec_seq) // 4)
    hp = layer_norm(x_prompt, ln_in_g, ln_in_b)
    hs = layer_norm(x_sample, ln_in_g, ln_in_b)
    rows_p, rows_s = [], []
    for layer in range(DEPTH):
        weights = (w_in[layer], b_in[layer], w_proj_a[layer], w_proj_b[layer], w_out[layer],
                   ln_g[layer], ln_b[layer], rel_bias)
        hp, new_p = trunk_layer(hp, None, pos_prompt, pos_prompt, True, n_sel_prompt, *weights)
        past = (cache_sb_k[layer], cache_sb_v[layer], cache_dsa_k[layer], cache_dsa_v[layer], cache_idx_k[layer])
        hs, new_s = trunk_layer(hs, past, pos_query_s, pos_keys_s, False, n_sel_sample, *weights)
        rows_p.append(new_p)
        rows_s.append(new_s)
    new_sb_k_p, new_sb_v_p, new_dsa_k_p, new_dsa_v_p, new_idx_k_p = (jnp.stack(r) for r in zip(*rows_p))
    new_sb_k_s, new_sb_v_s, new_dsa_k_s, new_dsa_v_s, new_idx_k_s = (jnp.stack(r) for r in zip(*rows_s))
    return (hp, hs, new_sb_k_p, new_sb_v_p, new_dsa_k_p, new_dsa_v_p, new_idx_k_p,
            new_sb_k_s, new_sb_v_s, new_dsa_k_s, new_dsa_v_s, new_idx_k_s)
```

```python
import functools
import math

import jax
import jax.numpy as jnp
import numpy as np
from jax import lax
from jax.experimental import pallas as pl
from jax.experimental.pallas import tpu as pltpu

F32 = jnp.float32
BF16 = jnp.bfloat16
I32 = jnp.int32

D_MODEL = 1024
DEPTH = 4
CHUNK = 64
H_A = 8
HD_A = 64
W_A = H_A * HD_A
H_B = 8
HD_B = 64
W_B = H_B * HD_B
H_IDX = 8
D_IDX = 64
MAX_SELECT = 256
N_BUCKETS = 32
MAX_DISTANCE = 128
LN_EPS = 1e-5
ALPHA = (2 * DEPTH) ** 0.25
SB_SCALE = HD_A ** -0.5
ATT_SCALE = HD_B ** -0.5
SPLIT_SIZES = (W_A, W_A, W_A, W_A, W_B, HD_B, HD_B, W_B, H_IDX * D_IDX, D_IDX, H_IDX, D_MODEL, D_MODEL)

LANES = 128
HALF = 64
NEG = -1e30
INT_MIN = -(2 ** 31)
V7X_VMEM_LIMIT = 56 * 1024 * 1024
ATT_BLOCK = 256
TOKEN_TILE = 512

C_QA, C_KA, C_VA, C_QB, C_QI = 0, 512, 1024, 1536, 2048
C_KB2, C_VB2, C_KI2, C_WI = 2560, 2688, 2816, 2944
N_ATT = 3072
C_GA, C_GB, C_RA, C_RB = 0, 512, 1024, 2048
N_GATE = 3072


def _params(sem):
    return pltpu.CompilerParams(dimension_semantics=sem, vmem_limit_bytes=V7X_VMEM_LIMIT)


def _nt_dot(a, b):
    return lax.dot_general(a, b, (((1,), (1,)), ((), ())), preferred_element_type=F32)


def _dot(a, b):
    return jnp.dot(a, b, preferred_element_type=F32)


def _lane_lo():
    return lax.broadcasted_iota(I32, (1, LANES), 1) < HALF


def _ln(x, g, b):
    mu = jnp.mean(x, axis=-1, keepdims=True)
    xc = x - mu
    var = jnp.mean(xc * xc, axis=-1, keepdims=True)
    return xc * lax.rsqrt(var + LN_EPS) * g + b


def _ln_kernel(x_ref, g_ref, b_ref, o_ref):
    o_ref[...] = _ln(x_ref[...], g_ref[...], b_ref[...])


def _layer_norm(x2d, g, b):
    n = x2d.shape[0]
    tm = min(TOKEN_TILE, n)
    return pl.pallas_call(
        _ln_kernel,
        grid=(n // tm,),
        in_specs=[pl.BlockSpec((tm, D_MODEL), lambda i: (i, 0)),
                  pl.BlockSpec((1, D_MODEL), lambda i: (0, 0)),
                  pl.BlockSpec((1, D_MODEL), lambda i: (0, 0))],
        out_specs=pl.BlockSpec((tm, D_MODEL), lambda i: (i, 0)),
        out_shape=jax.ShapeDtypeStruct((n, D_MODEL), F32),
        compiler_params=_params(("parallel",)),
    )(x2d, g.reshape(1, D_MODEL), b.reshape(1, D_MODEL))


def _proj_kernel(x_ref, w_ref, b_ref, qa_ref, ka_ref, va_ref, qb_ref, qi_ref,
                 kb2_ref, kb_ref, vb2_ref, vb_ref, ki2_ref, ki_ref, wi_ref):
    xb = x_ref[...].astype(BF16)

    def seg(c0, n):
        return _dot(xb, w_ref[:, c0:c0 + n]) + b_ref[:, c0:c0 + n]

    qa_ref[...] = seg(C_QA, W_A).astype(BF16)
    ka_ref[...] = seg(C_KA, W_A)
    va_ref[...] = seg(C_VA, W_A)
    qb_ref[...] = seg(C_QB, W_B).astype(BF16)
    qi_ref[...] = seg(C_QI, H_IDX * D_IDX).astype(BF16)
    kb = seg(C_KB2, LANES)
    kb2_ref[...] = kb.astype(BF16)
    kb_ref[...] = kb[:, :HD_B]
    vb = seg(C_VB2, LANES)
    vb2_ref[...] = vb.astype(BF16)
    vb_ref[...] = vb[:, :HD_B]
    ki = seg(C_KI2, LANES)
    ki2_ref[...] = ki.astype(BF16)
    ki_ref[...] = ki[:, :D_IDX]
    wi_ref[...] = seg(C_WI, LANES)


def _project(x2d, w_att, b_att):
    n = x2d.shape[0]
    tm = min(TOKEN_TILE, n)
    row = lambda width: pl.BlockSpec((tm, width), lambda i: (i, 0))
    full = lambda a: pl.BlockSpec(a.shape, lambda i: (0, 0))
    sds = lambda width, dt: jax.ShapeDtypeStruct((n, width), dt)
    return pl.pallas_call(
        _proj_kernel,
        grid=(n // tm,),
        in_specs=[row(D_MODEL), full(w_att), full(b_att)],
        out_specs=[row(W_A), row(W_A), row(W_A), row(W_B), row(H_IDX * D_IDX),
                   row(LANES), row(HD_B), row(LANES), row(HD_B), row(LANES), row(D_IDX), row(LANES)],
        out_shape=[sds(W_A, BF16), sds(W_A, F32), sds(W_A, F32), sds(W_B, BF16), sds(H_IDX * D_IDX, BF16),
                   sds(LANES, BF16), sds(HD_B, F32), sds(LANES, BF16), sds(HD_B, F32),
                   sds(LANES, BF16), sds(D_IDX, F32), sds(LANES, F32)],
        compiler_params=_params(("parallel",)),
    )(x2d, w_att, b_att)


def _sb_kernel(*refs, tb, off, past_len, has_past):
    if has_past:
        q_ref, kn_ref, vn_ref, kp_ref, vp_ref, o_ref, kbf, vlo, vhi, tri = refs
    else:
        q_ref, kn_ref, vn_ref, o_ref, kbf, vlo, vhi, tri = refs
    i = pl.program_id(2)
    lo = _lane_lo()

    @pl.when(i == 0)
    def _fill():
        def put(r0, k, v):
            n = k.shape[0]
            kbf[r0:r0 + n, :] = k.astype(BF16)
            vb = v.astype(BF16)
            vlo[r0:r0 + n, :] = jnp.where(lo, vb, jnp.zeros_like(vb))
            vhi[r0:r0 + n, :] = jnp.where(lo, jnp.zeros_like(vb), vb)

        if has_past:
            put(0, kp_ref[0], vp_ref[0])
        put(past_len, kn_ref[0], vn_ref[0])
        r = lax.broadcasted_iota(I32, (tb, tb), 0)
        c = lax.broadcasted_iota(I32, (tb, tb), 1)
        tri[...] = jnp.where(r > c, 1.0, 0.0).astype(BF16)

    q2 = q_ref[0]
    zero_q = jnp.zeros_like(q2)
    q_heads = (jnp.where(lo, q2, zero_q), jnp.where(lo, zero_q, q2))
    v_refs = (vlo, vhi)
    row = lax.broadcasted_iota(I32, (tb, tb), 0)
    col = lax.broadcasted_iota(I32, (tb, tb), 1)
    before = col < row
    qblk = i + off

    def block(j, carries, acc, diag):
        ks = pl.multiple_of(j * tb, tb)
        kblk = kbf[pl.ds(ks, tb), :]
        new_carries = []
        for par in range(2):
            z = _nt_dot(q_heads[par], kblk)
            sp = jnp.maximum(z, 0.0) + jnp.log(1.0 + jnp.exp(-jnp.abs(z)))
            lk = -sp
            if diag:
                lk = jnp.where(before, lk, 0.0)
            hi = lk.astype(BF16)
            lo_part = (lk - hi.astype(F32)).astype(BF16)
            suffix = _dot(hi, tri[...]) + _dot(lo_part, tri[...])
            a = jnp.exp((z - sp) + (suffix + carries[par]))
            if diag:
                a = jnp.where(before, a, 0.0)
            acc = acc + _dot(a.astype(BF16), v_refs[par][pl.ds(ks, tb), :])
            new_carries.append(carries[par] + (suffix[:, 0:1] + lk[:, 0:1]))
        return tuple(new_carries), acc

    zero_c = jnp.zeros((tb, 1), F32)
    carries, acc = block(qblk, (zero_c, zero_c), jnp.zeros((tb, LANES), F32), True)

    def body(t, state):
        c0, c1, acc = state
        (c0, c1), acc = block(qblk - 1 - t, (c0, c1), acc, False)
        return c0, c1, acc

    _, _, acc = lax.fori_loop(0, qblk, body, (carries[0], carries[1], acc))
    o_ref[0] = acc


def _stick_breaking(q, k_new, v_new, k_past, v_past):
    b, t, _ = q.shape
    has_past = k_past is not None
    p = k_past.shape[1] if has_past else 0
    tb = min(ATT_BLOCK, t)
    assert t % tb == 0 and p % tb == 0
    kv_new = pl.BlockSpec((1, t, LANES), lambda bi, hp, i: (bi, 0, hp))
    kv_past = pl.BlockSpec((1, p, LANES), lambda bi, hp, i: (bi, 0, hp))
    qo = pl.BlockSpec((1, tb, LANES), lambda bi, hp, i: (bi, i, hp))
    ins = [q, k_new, v_new] + ([k_past, v_past] if has_past else [])
    specs = [qo, kv_new, kv_new] + ([kv_past, kv_past] if has_past else [])
    length = p + t
    return pl.pallas_call(
        functools.partial(_sb_kernel, tb=tb, off=p // tb, past_len=p, has_past=has_past),
        grid=(b, W_A // LANES, t // tb),
        in_specs=specs,
        out_specs=qo,
        out_shape=jax.ShapeDtypeStruct((b, t, W_A), F32),
        scratch_shapes=[pltpu.VMEM((length, LANES), BF16), pltpu.VMEM((length, LANES), BF16),
                        pltpu.VMEM((length, LANES), BF16), pltpu.VMEM((tb, tb), BF16)],
        compiler_params=_params(("parallel", "parallel", "arbitrary")),
    )(*ins)


def _t5_bucket(rel):
    half = N_BUCKETS // 2
    max_exact = half // 2
    n = jnp.abs(rel)
    n_f = jnp.maximum(n, 1).astype(F32)
    large = max_exact + (jnp.log(n_f / max_exact) / math.log(MAX_DISTANCE / max_exact)
                         * (half - max_exact)).astype(I32)
    large = jnp.minimum(large, half - 1)
    return jnp.where(rel > 0, half, 0) + jnp.where(n < max_exact, n, large)


def _bias_kernel(rb_ref, o_ref, *, tb):
    h = pl.program_id(0)
    d = pl.program_id(1)
    r = lax.broadcasted_iota(I32, (tb, tb), 0)
    c = lax.broadcasted_iota(I32, (tb, tb), 1)
    bucket = _t5_bucket(c - r - d * tb)
    acc = jnp.zeros((tb, tb), F32)
    for bkt in range(N_BUCKETS):
        acc = jnp.where(bucket == bkt, rb_ref[bkt, h], acc)
    o_ref[0, 0] = acc


def _num_bias_diagonals(tb):
    return -(-MAX_DISTANCE // tb) + 1


def _bias_tiles(rel_bias, tb):
    nd = _num_bias_diagonals(tb)
    return pl.pallas_call(
        functools.partial(_bias_kernel, tb=tb),
        grid=(H_B, nd),
        in_specs=[pl.BlockSpec(memory_space=pltpu.SMEM)],
        out_specs=pl.BlockSpec((1, 1, tb, tb), lambda h, d: (h, d, 0, 0)),
        out_shape=jax.ShapeDtypeStruct((H_B, nd, tb, tb), F32),
        compiler_params=_params(("parallel", "parallel")),
    )(rel_bias)


def _dsa_kernel(*refs, tb, off, past_len, has_past, n_select, nd, length):
    if has_past:
        (qb_ref, qi_ref, wi_ref, kn_ref, vn_ref, in_ref, kp_ref, vp_ref, ip_ref, bt_ref, far_ref,
         o_ref, k2, vlo, vhi, i2, sc, mk, m_ref, l_ref, acc_ref) = refs
    else:
        (qb_ref, qi_ref, wi_ref, kn_ref, vn_ref, in_ref, bt_ref, far_ref,
         o_ref, k2, vlo, vhi, i2, sc, mk, m_ref, l_ref, acc_ref) = refs
    i = pl.program_id(1)
    lo = _lane_lo()
    qblk = i + off
    nkb = qblk + 1

    @pl.when(i == 0)
    def _fill():
        def put(r0, kd, vd, idd):
            n = kd.shape[0]
            k2[r0:r0 + n, :] = kd
            vlo[r0:r0 + n, :] = jnp.where(lo, vd, jnp.zeros_like(vd))
            vhi[r0:r0 + n, :] = jnp.where(lo, jnp.zeros_like(vd), vd)
            i2[r0:r0 + n, :] = idd

        if has_past:
            r = lax.broadcasted_iota(I32, (HALF, LANES), 0)
            c = lax.broadcasted_iota(I32, (HALF, LANES), 1)
            dup = jnp.where((c == r) | (c == r + HALF), 1.0, 0.0).astype(BF16)
            widen = lambda x: _dot(x.astype(BF16), dup).astype(BF16)
            put(0, widen(kp_ref[0]), widen(vp_ref[0]), widen(ip_ref[0]))
        put(past_len, kn_ref[0], vn_ref[0], in_ref[0])

    row = lax.broadcasted_iota(I32, (tb, tb), 0)
    col = lax.broadcasted_iota(I32, (tb, tb), 1)
    admissible = (col // CHUNK) <= (row // CHUNK)

    def head_queries(ref, hp):
        q2 = ref[0, :, hp * LANES:(hp + 1) * LANES]
        zero = jnp.zeros_like(q2)
        return jnp.where(lo, q2, zero), jnp.where(lo, zero, q2)

    wi = wi_ref[0]

    def score_block(j, diag):
        ks = pl.multiple_of(j * tb, tb)
        kib = i2[pl.ds(ks, tb), :]
        s = jnp.zeros((tb, tb), F32)
        for hp in range(H_IDX // 2):
            for par, qm in enumerate(head_queries(qi_ref, hp)):
                h = 2 * hp + par
                s = s + wi[:, h:h + 1] * jnp.maximum(_nt_dot(qm, kib), 0.0)
        bits = pltpu.bitcast(s, I32)
        key = bits ^ ((bits >> 31) & 0x7FFFFFFF)
        if diag:
            key = jnp.where(admissible, key, INT_MIN)
        sc[j] = key

    def score_body(j, _):
        score_block(j, False)
        return 0

    lax.fori_loop(0, qblk, score_body, 0)
    score_block(qblk, True)

    def count(pred_fn):
        def body(j, c):
            ks = pl.multiple_of(j * tb, tb)
            hit = pred_fn(sc[j], ks)
            return c + jnp.sum(jnp.where(hit, 1.0, 0.0), axis=-1, keepdims=True)
        return lax.fori_loop(0, nkb, body, jnp.zeros((tb, 1), F32))

    kf = float(n_select)

    def bit_body(t, tau):
        cand = tau + jnp.left_shift(jnp.int32(1), 31 - t)
        c = count(lambda blk, ks: blk >= cand)
        return jnp.where(c >= kf, cand, tau)

    tau = lax.fori_loop(0, 32, bit_body, jnp.full((tb, 1), INT_MIN, I32))
    cnt_ge = count(lambda blk, ks: blk >= tau)
    has_thr = tau > INT_MIN
    tie = has_thr & (cnt_ge > kf)
    any_tie = jnp.max(jnp.where(tie, 1.0, 0.0)) > 0.0

    def write_mask(sel_fn):
        def body(j, _):
            ks = pl.multiple_of(j * tb, tb)
            mk[j] = jnp.where(sel_fn(sc[j], ks), 0.0, NEG)
            return 0
        lax.fori_loop(0, nkb, body, 0)

    @pl.when(jnp.logical_not(any_tie))
    def _plain():
        thr = jnp.where(has_thr, tau, INT_MIN + 1)
        write_mask(lambda blk, ks: blk >= thr)

    @pl.when(any_tie)
    def _ties():
        need = kf - count(lambda blk, ks: blk > tau)
        q_idx = jnp.zeros((tb, 1), I32)
        for bit in reversed(range(max(1, (length - 1).bit_length()))):
            cand = q_idx + (1 << bit)
            c = count(lambda blk, ks: (blk == tau) & ((col + ks) < cand))
            q_idx = jnp.where(c < need, cand, q_idx)
        last_eq = jnp.where(has_thr, jnp.where(tie, q_idx, length), -1)
        write_mask(lambda blk, ks: (blk > tau) | ((blk == tau) & ((col + ks) <= last_eq)))

    m_ref[...] = jnp.full(m_ref.shape, NEG, F32)
    l_ref[...] = jnp.zeros(l_ref.shape, F32)
    acc_ref[...] = jnp.zeros(acc_ref.shape, F32)

    def attend(j, bias_fn):
        ks = pl.multiple_of(j * tb, tb)
        kblk = k2[pl.ds(ks, tb), :]
        mblk = mk[j]
        for hp in range(H_B // 2):
            alphas = []
            pv = None
            for par, qm in enumerate(head_queries(qb_ref, hp)):
                h = 2 * hp + par
                s = _nt_dot(qm, kblk) + bias_fn(h) + mblk
                m_prev = m_ref[h]
                m_new = jnp.maximum(m_prev, jnp.max(s, axis=-1, keepdims=True))
                alpha = jnp.exp(m_prev - m_new)
                p = jnp.exp(s - m_new[:, 0:1])
                l_ref[h] = alpha * l_ref[h] + jnp.sum(p, axis=-1, keepdims=True)
                m_ref[h] = m_new
                vref = vlo if par == 0 else vhi
                contrib = _dot(p.astype(BF16), vref[pl.ds(ks, tb), :])
                pv = contrib if pv is None else pv + contrib
                alphas.append(alpha)
            acc_ref[hp] = acc_ref[hp] * jnp.where(lo, alphas[0], alphas[1]) + pv

    def far_body(j, _):
        attend(j, lambda h: far_ref[h])
        return 0

    lax.fori_loop(0, jnp.maximum(qblk - nd + 1, 0), far_body, 0)
    for d in range(nd):
        @pl.when(qblk - d >= 0)
        def _near(d=d):
            attend(qblk - d, lambda h: bt_ref[h, d])

    for hp in range(H_B // 2):
        denom = jnp.where(lo, l_ref[2 * hp], l_ref[2 * hp + 1])
        o_ref[0, :, hp * LANES:(hp + 1) * LANES] = acc_ref[hp] / denom


def _sparse_attention(q_b, q_i, w_i, k2_new, v2_new, i2_new, k_past, v_past, i_past, bias_tiles, bias_far):
    b, t, _ = q_b.shape
    has_past = k_past is not None
    p = k_past.shape[1] if has_past else 0
    tb = min(ATT_BLOCK, t)
    assert t % tb == 0 and p % tb == 0 and tb % CHUNK == 0 and bias_tiles.shape[2] == tb
    length = p + t
    n_select = min(MAX_SELECT, length // 4)
    nd = bias_tiles.shape[1]
    qspec = pl.BlockSpec((1, tb, W_B), lambda bi, i: (bi, i, 0))
    new = pl.BlockSpec((1, t, LANES), lambda bi, i: (bi, 0, 0))
    past = pl.BlockSpec((1, p, HD_B), lambda bi, i: (bi, 0, 0))
    ins = [q_b, q_i, w_i, k2_new, v2_new, i2_new] + ([k_past, v_past, i_past] if has_past else [])
    specs = ([qspec, qspec, pl.BlockSpec((1, tb, LANES), lambda bi, i: (bi, i, 0)), new, new, new]
             + ([past, past, past] if has_past else []))
    ins += [bias_tiles, bias_far]
    specs += [pl.BlockSpec(bias_tiles.shape, lambda bi, i: (0, 0, 0, 0)), pl.BlockSpec(memory_space=pltpu.SMEM)]
    kv = lambda: pltpu.VMEM((length, LANES), BF16)
    return pl.pallas_call(
        functools.partial(_dsa_kernel, tb=tb, off=p // tb, past_len=p, has_past=has_past,
                          n_select=n_select, nd=nd, length=length),
        grid=(b, t // tb),
        in_specs=specs,
        out_specs=qspec,
        out_shape=jax.ShapeDtypeStruct((b, t, W_B), F32),
        scratch_shapes=[kv(), kv(), kv(), kv(),
                        pltpu.VMEM((length // tb, tb, tb), I32), pltpu.VMEM((length // tb, tb, tb), F32),
                        pltpu.VMEM((H_B, tb, 1), F32), pltpu.VMEM((H_B, tb, 1), F32),
                        pltpu.VMEM((H_B // 2, tb, LANES), F32)],
        compiler_params=_params(("parallel", "arbitrary")),
    )(*ins)


def _sigmoid(x):
    return 1.0 / (1.0 + jnp.exp(-x))


def _merge_kernel(x_ref, ya_ref, yb_ref, wg_ref, bg_ref, wpa_ref, wpb_ref, wo_ref, g_ref, b_ref, o_ref):
    x = x_ref[...]
    xb = x.astype(BF16)

    def seg(c0, n):
        return _dot(xb, wg_ref[:, c0:c0 + n]) + bg_ref[:, c0:c0 + n]

    g_a = seg(C_GA, W_A)
    y_a = (ya_ref[...] * (g_a * _sigmoid(g_a))).astype(BF16)
    branch_a = _sigmoid(seg(C_RA, D_MODEL)) * _dot(y_a, wpa_ref[...])
    g_b = seg(C_GB, W_B)
    y_b = (yb_ref[...] * (g_b * _sigmoid(g_b))).astype(BF16)
    branch_b = _sigmoid(seg(C_RB, D_MODEL)) * _dot(y_b, wpb_ref[...])
    merged = (branch_a + branch_b).astype(BF16)
    o_ref[...] = _ln(ALPHA * x + _dot(merged, wo_ref[...]), g_ref[...], b_ref[...])


def _merge(x2d, y_a, y_b, w_gate, b_gate, w_pa, w_pb, w_out, ln_g, ln_b):
    n = x2d.shape[0]
    tm = min(TOKEN_TILE, n)
    row = lambda width: pl.BlockSpec((tm, width), lambda i: (i, 0))
    full = lambda a: pl.BlockSpec(a.shape, lambda i: (0, 0))
    consts = [w_gate, b_gate, w_pa, w_pb, w_out, ln_g, ln_b]
    return pl.pallas_call(
        _merge_kernel,
        grid=(n // tm,),
        in_specs=[row(D_MODEL), row(W_A), row(W_B)] + [full(a) for a in consts],
        out_specs=row(D_MODEL),
        out_shape=jax.ShapeDtypeStruct((n, D_MODEL), F32),
        compiler_params=_params(("parallel",)),
    )(x2d, y_a, y_b, *consts)


def _pack_weights(w_in, b_in):
    offs = np.concatenate([[0], np.cumsum(SPLIT_SIZES)])
    names = ("q_a", "k_a", "v_a", "g_a", "q_b", "k_b", "v_b", "g_b", "q_i", "k_i", "w_i", "r_a", "r_b")
    w = {nm: w_in[:, :, offs[k]:offs[k + 1]] for k, nm in enumerate(names)}
    b = {nm: b_in[:, offs[k]:offs[k + 1]] for k, nm in enumerate(names)}
    pad = LANES - H_IDX

    def build(parts, last):
        order = (parts["q_a"] * SB_SCALE, parts["k_a"], parts["v_a"], parts["q_b"] * ATT_SCALE, parts["q_i"],
                 parts["k_b"], parts["k_b"], parts["v_b"], parts["v_b"], parts["k_i"], parts["k_i"], last)
        return jnp.concatenate(order, axis=-1)

    w_att = build(w, jnp.pad(w["w_i"], ((0, 0), (0, 0), (0, pad)))).astype(BF16)
    b_att = build(b, jnp.pad(b["w_i"], ((0, 0), (0, pad))))[:, None, :]
    w_gate = jnp.concatenate([w["g_a"], w["g_b"], w["r_a"], w["r_b"]], axis=-1).astype(BF16)
    b_gate = jnp.concatenate([b["g_a"], b["g_b"], b["r_a"], b["r_b"]], axis=-1)[:, None, :]
    return w_att, b_att, w_gate, b_gate


def _trunk_layer(x2d, batch, seq, past, weights, bias_tiles, bias_far):
    w_att, b_att, w_gate, b_gate, w_pa, w_pb, w_out, ln_g, ln_b = weights
    (q_a, k_a, v_a, q_b, q_i, kb2, k_b, vb2, v_b, ki2, k_i, w_i) = _project(x2d, w_att, b_att)
    shape3 = lambda a: a.reshape(batch, seq, a.shape[-1])
    if past is None:
        pk_a = pv_a = pk_b = pv_b = pk_i = None
    else:
        pk_a, pv_a, pk_b, pv_b, pk_i = past
        pk_a = pk_a.reshape(batch, -1, W_A)
        pv_a = pv_a.reshape(batch, -1, W_A)
    y_a = _stick_breaking(shape3(q_a), shape3(k_a), shape3(v_a), pk_a, pv_a)
    y_b = _sparse_attention(shape3(q_b), shape3(q_i), shape3(w_i), shape3(kb2), shape3(vb2), shape3(ki2),
                            pk_b, pv_b, pk_i, bias_tiles, bias_far)
    x_next = _merge(x2d, y_a.reshape(-1, W_A), y_b.reshape(-1, W_B),
                    w_gate, b_gate, w_pa, w_pb, w_out, ln_g, ln_b)
    new_rows = (k_a.reshape(batch, seq, H_A, HD_A), v_a.reshape(batch, seq, H_A, HD_A),
                shape3(k_b), shape3(v_b), shape3(k_i))
    return x_next, new_rows


def kernel(x_prompt, x_sample, cache_sb_k, cache_sb_v, cache_dsa_k, cache_dsa_v, cache_idx_k,
           ln_in_g, ln_in_b, w_in, b_in, w_proj_a, w_proj_b, w_out, ln_g, ln_b, rel_bias):
    batch, seq, _ = x_prompt.shape
    dec_batch, dec_seq, _ = x_sample.shape
    w_att, b_att, w_gate, b_gate = _pack_weights(w_in, b_in)
    w_pa = w_proj_a.astype(BF16)
    w_pb = w_proj_b.astype(BF16)
    w_o = w_out.astype(BF16)
    bias_far = rel_bias[N_BUCKETS // 2 - 1]
    tiles_p = _bias_tiles(rel_bias, min(ATT_BLOCK, seq))
    tiles_s = _bias_tiles(rel_bias, min(ATT_BLOCK, dec_seq))
    hp = _layer_norm(x_prompt.reshape(-1, D_MODEL), ln_in_g, ln_in_b)
    hs = _layer_norm(x_sample.reshape(-1, D_MODEL), ln_in_g, ln_in_b)
    rows_p, rows_s = [], []
    for layer in range(DEPTH):
        weights = (w_att[layer], b_att[layer], w_gate[layer], b_gate[layer], w_pa[layer], w_pb[layer], w_o[layer],
                   ln_g[layer].reshape(1, D_MODEL), ln_b[layer].reshape(1, D_MODEL))
        hp, new_p = _trunk_layer(hp, batch, seq, None, weights, tiles_p, bias_far)
        past = (cache_sb_k[layer], cache_sb_v[layer], cache_dsa_k[layer], cache_dsa_v[layer], cache_idx_k[layer])
        hs, new_s = _trunk_layer(hs, dec_batch, dec_seq, past, weights, tiles_s, bias_far)
        rows_p.append(new_p)
        rows_s.append(new_s)
    stacked_p = tuple(jnp.stack(r) for r in zip(*rows_p))
    stacked_s = tuple(jnp.stack(r) for r in zip(*rows_s))
    return (hp.reshape(batch, seq, D_MODEL), hs.reshape(dec_batch, dec_seq, D_MODEL)) + stacked_p + stacked_s
```

```python
import functools
import math

import jax
import jax.numpy as jnp
import numpy as np
from jax import lax
from jax.experimental import pallas as pl
from jax.experimental.pallas import tpu as pltpu

F32 = jnp.float32
BF16 = jnp.bfloat16
I32 = jnp.int32

D_MODEL = 1024
DEPTH = 4
CHUNK = 64
H_A = 8
HD_A = 64
W_A = H_A * HD_A
H_B = 8
HD_B = 64
W_B = H_B * HD_B
H_IDX = 8
D_IDX = 64
MAX_SELECT = 256
N_BUCKETS = 32
MAX_DISTANCE = 128
LN_EPS = 1e-5
ALPHA = (2 * DEPTH) ** 0.25
SB_SCALE = HD_A ** -0.5
ATT_SCALE = HD_B ** -0.5
SPLIT_SIZES = (W_A, W_A, W_A, W_A, W_B, HD_B, HD_B, W_B, H_IDX * D_IDX, D_IDX, H_IDX, D_MODEL, D_MODEL)

LANES = 128
SUBLANES = 8
HALF = 64
NEG = -1e30
INT_MIN = -(2 ** 31)
EXP_UNDERFLOW = -105.0
V7X_VMEM_LIMIT = 56 * 1024 * 1024
ATT_BLOCK = 256
TOKEN_TILE = 512

C_QA, C_KA, C_VA, C_QB, C_QI = 0, 512, 1024, 1536, 2048
C_KB2, C_VB2, C_KI2, C_WI = 2560, 2688, 2816, 2944
N_ATT = 3072
C_GA, C_GB, C_RA, C_RB = 0, 512, 1024, 2048
N_GATE = 3072


def _params(sem):
    return pltpu.CompilerParams(dimension_semantics=sem, vmem_limit_bytes=V7X_VMEM_LIMIT)


def _nt_dot(a, b):
    return lax.dot_general(a, b, (((1,), (1,)), ((), ())), preferred_element_type=F32)


def _dot(a, b):
    return jnp.dot(a, b, preferred_element_type=F32)


def _lane_lo():
    return lax.broadcasted_iota(I32, (1, LANES), 1) < HALF


def _ln(x, g, b):
    mu = jnp.mean(x, axis=-1, keepdims=True)
    xc = x - mu
    var = jnp.mean(xc * xc, axis=-1, keepdims=True)
    return xc * lax.rsqrt(var + LN_EPS) * g + b


def _ln_kernel(x_ref, g_ref, b_ref, o_ref):
    o_ref[...] = _ln(x_ref[...], g_ref[...], b_ref[...])


def _layer_norm(x2d, g, b):
    n = x2d.shape[0]
    tm = min(TOKEN_TILE, n)
    return pl.pallas_call(
        _ln_kernel,
        grid=(n // tm,),
        in_specs=[pl.BlockSpec((tm, D_MODEL), lambda i: (i, 0)),
                  pl.BlockSpec((1, D_MODEL), lambda i: (0, 0)),
                  pl.BlockSpec((1, D_MODEL), lambda i: (0, 0))],
        out_specs=pl.BlockSpec((tm, D_MODEL), lambda i: (i, 0)),
        out_shape=jax.ShapeDtypeStruct((n, D_MODEL), F32),
        compiler_params=_params(("parallel",)),
        name=f"ln_n{n}",
    )(x2d, g.reshape(1, D_MODEL), b.reshape(1, D_MODEL))


def _proj_kernel(x_ref, w_ref, b_ref, qa_ref, ka_ref, va_ref, qb_ref, qi_ref,
                 kb2_ref, kb_ref, vb2_ref, vb_ref, ki2_ref, ki_ref, wi_ref):
    xb = x_ref[...].astype(BF16)

    def seg(c0, n):
        return _dot(xb, w_ref[:, c0:c0 + n]) + b_ref[:, c0:c0 + n]

    qa_ref[...] = seg(C_QA, W_A).astype(BF16)
    ka_ref[...] = seg(C_KA, W_A)
    va_ref[...] = seg(C_VA, W_A)
    qb_ref[...] = seg(C_QB, W_B).astype(BF16)
    qi_ref[...] = seg(C_QI, H_IDX * D_IDX).astype(BF16)
    kb = seg(C_KB2, LANES)
    kb2_ref[...] = kb.astype(BF16)
    kb_ref[...] = kb[:, :HD_B]
    vb = seg(C_VB2, LANES)
    vb2_ref[...] = vb.astype(BF16)
    vb_ref[...] = vb[:, :HD_B]
    ki = seg(C_KI2, LANES)
    ki2_ref[...] = ki.astype(BF16)
    ki_ref[...] = ki[:, :D_IDX]
    wi_ref[...] = seg(C_WI, LANES)


def _project(x2d, w_att, b_att):
    n = x2d.shape[0]
    tm = min(TOKEN_TILE, n)
    row = lambda width: pl.BlockSpec((tm, width), lambda i: (i, 0))
    full = lambda a: pl.BlockSpec(a.shape, lambda i: (0, 0))
    sds = lambda width, dt: jax.ShapeDtypeStruct((n, width), dt)
    return pl.pallas_call(
        _proj_kernel,
        grid=(n // tm,),
        in_specs=[row(D_MODEL), full(w_att), full(b_att)],
        out_specs=[row(W_A), row(W_A), row(W_A), row(W_B), row(H_IDX * D_IDX),
                   row(LANES), row(HD_B), row(LANES), row(HD_B), row(LANES), row(D_IDX), row(LANES)],
        out_shape=[sds(W_A, BF16), sds(W_A, F32), sds(W_A, F32), sds(W_B, BF16), sds(H_IDX * D_IDX, BF16),
                   sds(LANES, BF16), sds(HD_B, F32), sds(LANES, BF16), sds(HD_B, F32),
                   sds(LANES, BF16), sds(D_IDX, F32), sds(LANES, F32)],
        compiler_params=_params(("parallel",)),
        name=f"proj_n{n}",
    )(x2d, w_att, b_att)


def _sb_kernel(*refs, tb, off, past_len, has_past):
    if has_past:
        q_ref, kn_ref, vn_ref, kp_ref, vp_ref, o_ref, kbf, vlo, vhi, tri = refs
    else:
        q_ref, kn_ref, vn_ref, o_ref, kbf, vlo, vhi, tri = refs
    i = pl.program_id(2)
    lo = _lane_lo()

    @pl.when(i == 0)
    def _fill():
        def put(r0, k, v):
            n = k.shape[0]
            kbf[r0:r0 + n, :] = k.astype(BF16)
            vb = v.astype(BF16)
            vlo[r0:r0 + n, :] = jnp.where(lo, vb, jnp.zeros_like(vb))
            vhi[r0:r0 + n, :] = jnp.where(lo, jnp.zeros_like(vb), vb)

        if has_past:
            put(0, kp_ref[0], vp_ref[0])
        put(past_len, kn_ref[0], vn_ref[0])
        r = lax.broadcasted_iota(I32, (tb, tb), 0)
        c = lax.broadcasted_iota(I32, (tb, tb), 1)
        tri[...] = jnp.where(r > c, 1.0, 0.0).astype(BF16)

    q2 = q_ref[0]
    zero_q = jnp.zeros_like(q2)
    q_heads = (jnp.where(lo, q2, zero_q), jnp.where(lo, zero_q, q2))
    v_refs = (vlo, vhi)
    row = lax.broadcasted_iota(I32, (tb, tb), 0)
    col = lax.broadcasted_iota(I32, (tb, tb), 1)
    before = col < row
    qblk = i + off

    def block(j, carries, acc, diag):
        ks = pl.multiple_of(j * tb, tb)
        kblk = kbf[pl.ds(ks, tb), :]
        new_carries = []
        for par in range(2):
            z = _nt_dot(q_heads[par], kblk)
            sp = jnp.maximum(z, 0.0) + jnp.log(1.0 + jnp.exp(-jnp.abs(z)))
            lk = -sp
            if diag:
                lk = jnp.where(before, lk, 0.0)
            hi = lk.astype(BF16)
            lo_part = (lk - hi.astype(F32)).astype(BF16)
            suffix = _dot(hi, tri[...]) + _dot(lo_part, tri[...])
            a = jnp.exp((z - sp) + (suffix + carries[par]))
            if diag:
                a = jnp.where(before, a, 0.0)
            acc = acc + _dot(a.astype(BF16), v_refs[par][pl.ds(ks, tb), :])
            new_carries.append(carries[par] + (suffix[:, 0:1] + lk[:, 0:1]))
        return tuple(new_carries), acc

    def any_weight_left(c0, c1):
        return (jnp.max(jnp.maximum(c0, c1)) > EXP_UNDERFLOW).astype(I32)

    zero_c = jnp.zeros((tb, 1), F32)
    (c0, c1), acc = block(qblk, (zero_c, zero_c), jnp.zeros((tb, LANES), F32), True)

    def cond(state):
        t, _, _, _, live = state
        return jnp.logical_and(t < qblk, live > 0)

    def body(state):
        t, c0, c1, acc, _ = state
        (c0, c1), acc = block(qblk - 1 - t, (c0, c1), acc, False)
        return t + 1, c0, c1, acc, any_weight_left(c0, c1)

    state = lax.while_loop(cond, body, (jnp.int32(0), c0, c1, acc, any_weight_left(c0, c1)))
    o_ref[0] = state[3]


def _stick_breaking(q, k_new, v_new, k_past, v_past):
    b, t, _ = q.shape
    has_past = k_past is not None
    p = k_past.shape[1] if has_past else 0
    tb = ATT_BLOCK
    assert t % tb == 0 and p % tb == 0
    kv_new = pl.BlockSpec((1, t, LANES), lambda bi, hp, i: (bi, 0, hp))
    kv_past = pl.BlockSpec((1, p, LANES), lambda bi, hp, i: (bi, 0, hp))
    qo = pl.BlockSpec((1, tb, LANES), lambda bi, hp, i: (bi, i, hp))
    ins = [q, k_new, v_new] + ([k_past, v_past] if has_past else [])
    specs = [qo, kv_new, kv_new] + ([kv_past, kv_past] if has_past else [])
    length = p + t
    return pl.pallas_call(
        functools.partial(_sb_kernel, tb=tb, off=p // tb, past_len=p, has_past=has_past),
        grid=(b, W_A // LANES, t // tb),
        in_specs=specs,
        out_specs=qo,
        out_shape=jax.ShapeDtypeStruct((b, t, W_A), F32),
        scratch_shapes=[pltpu.VMEM((length, LANES), BF16), pltpu.VMEM((length, LANES), BF16),
                        pltpu.VMEM((length, LANES), BF16), pltpu.VMEM((tb, tb), BF16)],
        compiler_params=_params(("parallel", "parallel", "arbitrary")),
        name=f"sb_t{t}",
    )(*ins)


def _t5_bucket(rel):
    half = N_BUCKETS // 2
    max_exact = half // 2
    n = jnp.abs(rel)
    n_f = jnp.maximum(n, 1).astype(F32)
    large = max_exact + (jnp.log(n_f / max_exact) / math.log(MAX_DISTANCE / max_exact)
                         * (half - max_exact)).astype(I32)
    large = jnp.minimum(large, half - 1)
    return jnp.where(rel > 0, half, 0) + jnp.where(n < max_exact, n, large)


def _bias_kernel(rb_ref, o_ref, *, tb):
    h = pl.program_id(0)
    d = pl.program_id(1)
    key = lax.broadcasted_iota(I32, (tb, tb), 0)
    query = lax.broadcasted_iota(I32, (tb, tb), 1)
    bucket = _t5_bucket(key - query - d * tb)
    acc = jnp.zeros((tb, tb), F32)
    for bkt in range(N_BUCKETS):
        acc = jnp.where(bucket == bkt, rb_ref[bkt, h], acc)
    o_ref[0, 0] = acc


def _num_bias_diagonals(tb):
    return -(-MAX_DISTANCE // tb) + 1


def _bias_tiles(rel_bias, tb):
    nd = _num_bias_diagonals(tb)
    return pl.pallas_call(
        functools.partial(_bias_kernel, tb=tb),
        grid=(H_B, nd),
        in_specs=[pl.BlockSpec(memory_space=pltpu.SMEM)],
        out_specs=pl.BlockSpec((1, 1, tb, tb), lambda h, d: (h, d, 0, 0)),
        out_shape=jax.ShapeDtypeStruct((H_B, nd, tb, tb), F32),
        compiler_params=_params(("parallel", "parallel")),
        name=f"bias_tb{tb}",
    )(rel_bias)


def _dsa_kernel(*refs, tb, off, past_len, has_past, n_select, nd, length):
    if has_past:
        (qb_ref, qi_ref, wi_ref, kn_ref, vn_ref, in_ref, kp_ref, vp_ref, ip_ref, bt_ref, far_ref,
         o_ref, k2, vt_lo, vt_hi, i2, sc, mk, m_ref, l_ref, acc_ref) = refs
    else:
        (qb_ref, qi_ref, wi_ref, kn_ref, vn_ref, in_ref, bt_ref, far_ref,
         o_ref, k2, vt_lo, vt_hi, i2, sc, mk, m_ref, l_ref, acc_ref) = refs
    i = pl.program_id(1)
    lo = _lane_lo()
    qblk = i + off
    nkb = qblk + 1

    @pl.when(i == 0)
    def _fill():
        r = lax.broadcasted_iota(I32, (LANES, LANES), 0)
        c = lax.broadcasted_iota(I32, (LANES, LANES), 1)
        eye_lo = jnp.where((r == c) & (r < HALF), 1.0, 0.0).astype(BF16)
        eye_hi = jnp.where((r == c) & (r >= HALF), 1.0, 0.0).astype(BF16)

        def put(j0, kd, vd, idd):
            n = kd.shape[0]
            k2[j0 * tb:j0 * tb + n, :] = kd
            i2[j0 * tb:j0 * tb + n, :] = idd
            for jb in range(n // tb):
                blk = vd[jb * tb:(jb + 1) * tb, :]
                vt_lo[j0 + jb] = _nt_dot(eye_lo, blk).astype(BF16)
                vt_hi[j0 + jb] = _nt_dot(eye_hi, blk).astype(BF16)

        if has_past:
            rr = lax.broadcasted_iota(I32, (HALF, LANES), 0)
            cc = lax.broadcasted_iota(I32, (HALF, LANES), 1)
            dup = jnp.where((cc == rr) | (cc == rr + HALF), 1.0, 0.0).astype(BF16)
            widen = lambda x: _dot(x.astype(BF16), dup).astype(BF16)
            put(0, widen(kp_ref[0]), widen(vp_ref[0]), widen(ip_ref[0]))
        put(past_len // tb, kn_ref[0], vn_ref[0], in_ref[0])

    key_l = lax.broadcasted_iota(I32, (tb, tb), 0)
    query_l = lax.broadcasted_iota(I32, (tb, tb), 1)
    admissible = (key_l // CHUNK) <= (query_l // CHUNK)

    def head_queries(ref, hp):
        q2 = ref[0, :, hp * LANES:(hp + 1) * LANES]
        zero = jnp.zeros_like(q2)
        return jnp.where(lo, q2, zero), jnp.where(lo, zero, q2)

    wi_t = wi_ref[0].T

    def score_block(j, diag):
        ks = pl.multiple_of(j * tb, tb)
        kib = i2[pl.ds(ks, tb), :]
        s = jnp.zeros((tb, tb), F32)
        for hp in range(H_IDX // 2):
            for par, qm in enumerate(head_queries(qi_ref, hp)):
                h = 2 * hp + par
                s = s + wi_t[h:h + 1, :] * jnp.maximum(_nt_dot(kib, qm), 0.0)
        bits = pltpu.bitcast(s, I32)
        key = bits ^ ((bits >> 31) & 0x7FFFFFFF)
        if diag:
            key = jnp.where(admissible, key, INT_MIN)
        sc[j] = key

    def score_body(j, _):
        score_block(j, False)
        return 0

    lax.fori_loop(0, qblk, score_body, 0)
    score_block(qblk, True)

    def count(pred_fn):
        def body(j, c):
            hit = pred_fn(sc[j], j * tb)
            ones = jnp.where(hit, 1.0, 0.0)
            return c + jnp.sum(ones.reshape(tb // SUBLANES, SUBLANES, tb), axis=0)
        c8 = lax.fori_loop(0, nkb, body, jnp.zeros((SUBLANES, tb), F32))
        return jnp.sum(c8, axis=0, keepdims=True)

    kf = float(n_select)

    def bit_body(t, tau):
        cand = tau + jnp.left_shift(jnp.int32(1), 31 - t)
        c = count(lambda blk, ks: blk >= cand)
        return jnp.where(c >= kf, cand, tau)

    tau = lax.fori_loop(0, 32, bit_body, jnp.full((1, tb), INT_MIN, I32))
    cnt_ge = count(lambda blk, ks: blk >= tau)
    has_thr = tau > INT_MIN
    tie = has_thr & (cnt_ge > kf)
    any_tie = jnp.max(jnp.where(tie, 1.0, 0.0)) > 0.0

    def write_mask(sel_fn):
        def body(j, _):
            mk[j] = jnp.where(sel_fn(sc[j], j * tb), 0.0, NEG)
            return 0
        lax.fori_loop(0, nkb, body, 0)

    @pl.when(jnp.logical_not(any_tie))
    def _plain():
        thr = jnp.where(has_thr, tau, INT_MIN + 1)
        write_mask(lambda blk, ks: blk >= thr)

    @pl.when(any_tie)
    def _ties():
        need = kf - count(lambda blk, ks: blk > tau)
        q_idx = jnp.zeros((1, tb), I32)
        for bit in reversed(range(max(1, (length - 1).bit_length()))):
            cand = q_idx + (1 << bit)
            c = count(lambda blk, ks: (blk == tau) & ((key_l + ks) < cand))
            q_idx = jnp.where(c < need, cand, q_idx)
        last_eq = jnp.where(has_thr, jnp.where(tie, q_idx, length), -1)
        write_mask(lambda blk, ks: (blk > tau) | ((blk == tau) & ((key_l + ks) <= last_eq)))

    m_ref[...] = jnp.full(m_ref.shape, NEG, F32)
    l_ref[...] = jnp.zeros(l_ref.shape, F32)
    acc_ref[...] = jnp.zeros(acc_ref.shape, F32)
    row_lo = lax.broadcasted_iota(I32, (LANES, 1), 0) < HALF

    def attend(j, bias_fn):
        ks = pl.multiple_of(j * tb, tb)
        kblk = k2[pl.ds(ks, tb), :]
        mblk = mk[j]
        for hp in range(H_B // 2):
            alphas = []
            pv = None
            for par, qm in enumerate(head_queries(qb_ref, hp)):
                h = 2 * hp + par
                s = _nt_dot(kblk, qm) + bias_fn(h) + mblk
                m_prev = m_ref[h]
                m_new = jnp.maximum(m_prev, jnp.max(s, axis=0, keepdims=True))
                alpha = jnp.exp(m_prev - m_new)
                p = jnp.exp(s - m_new)
                l_ref[h] = alpha * l_ref[h] + jnp.sum(p, axis=0, keepdims=True)
                m_ref[h] = m_new
                vt = (vt_lo if par == 0 else vt_hi)[j]
                contrib = _dot(vt, p.astype(BF16))
                pv = contrib if pv is None else pv + contrib
                alphas.append(alpha)
            acc_ref[hp] = acc_ref[hp] * jnp.where(row_lo, alphas[0], alphas[1]) + pv

    def far_body(j, _):
        attend(j, lambda h: far_ref[h])
        return 0

    lax.fori_loop(0, jnp.maximum(qblk - nd + 1, 0), far_body, 0)
    for d in range(nd):
        @pl.when(qblk - d >= 0)
        def _near(d=d):
            attend(qblk - d, lambda h: bt_ref[h, d])

    for hp in range(H_B // 2):
        denom = jnp.where(row_lo, l_ref[2 * hp], l_ref[2 * hp + 1])
        o_ref[0, :, hp * LANES:(hp + 1) * LANES] = (acc_ref[hp] / denom).T


def _sparse_attention(q_b, q_i, w_i, k2_new, v2_new, i2_new, k_past, v_past, i_past, bias_tiles, bias_far,
                      n_select):
    b, t, _ = q_b.shape
    has_past = k_past is not None
    p = k_past.shape[1] if has_past else 0
    tb = ATT_BLOCK
    assert t % tb == 0 and p % tb == 0 and tb % CHUNK == 0 and bias_tiles.shape[2] == tb
    length = p + t
    nd = bias_tiles.shape[1]
    nb = length // tb
    qspec = pl.BlockSpec((1, tb, W_B), lambda bi, i: (bi, i, 0))
    new = pl.BlockSpec((1, t, LANES), lambda bi, i: (bi, 0, 0))
    past = pl.BlockSpec((1, p, HD_B), lambda bi, i: (bi, 0, 0))
    ins = [q_b, q_i, w_i, k2_new, v2_new, i2_new] + ([k_past, v_past, i_past] if has_past else [])
    specs = ([qspec, qspec, pl.BlockSpec((1, tb, LANES), lambda bi, i: (bi, i, 0)), new, new, new]
             + ([past, past, past] if has_past else []))
    ins += [bias_tiles, bias_far]
    specs += [pl.BlockSpec(bias_tiles.shape, lambda bi, i: (0, 0, 0, 0)), pl.BlockSpec(memory_space=pltpu.SMEM)]
    return pl.pallas_call(
        functools.partial(_dsa_kernel, tb=tb, off=p // tb, past_len=p, has_past=has_past,
                          n_select=n_select, nd=nd, length=length),
        grid=(b, t // tb),
        in_specs=specs,
        out_specs=qspec,
        out_shape=jax.ShapeDtypeStruct((b, t, W_B), F32),
        scratch_shapes=[pltpu.VMEM((length, LANES), BF16),
                        pltpu.VMEM((nb, LANES, tb), BF16), pltpu.VMEM((nb, LANES, tb), BF16),
                        pltpu.VMEM((length, LANES), BF16),
                        pltpu.VMEM((nb, tb, tb), I32), pltpu.VMEM((nb, tb, tb), F32),
                        pltpu.VMEM((H_B, 1, tb), F32), pltpu.VMEM((H_B, 1, tb), F32),
                        pltpu.VMEM((H_B // 2, LANES, tb), F32)],
        compiler_params=_params(("parallel", "arbitrary")),
        name=f"dsa_t{t}",
    )(*ins)


def _sigmoid(x):
    return 1.0 / (1.0 + jnp.exp(-x))


def _merge_kernel(x_ref, ya_ref, yb_ref, wg_ref, bg_ref, wpa_ref, wpb_ref, wo_ref, g_ref, b_ref, o_ref):
    x = x_ref[...]
    xb = x.astype(BF16)

    def seg(c0, n):
        return _dot(xb, wg_ref[:, c0:c0 + n]) + bg_ref[:, c0:c0 + n]

    g_a = seg(C_GA, W_A)
    y_a = (ya_ref[...] * (g_a * _sigmoid(g_a))).astype(BF16)
    branch_a = _sigmoid(seg(C_RA, D_MODEL)) * _dot(y_a, wpa_ref[...])
    g_b = seg(C_GB, W_B)
    y_b = (yb_ref[...] * (g_b * _sigmoid(g_b))).astype(BF16)
    branch_b = _sigmoid(seg(C_RB, D_MODEL)) * _dot(y_b, wpb_ref[...])
    merged = (branch_a + branch_b).astype(BF16)
    o_ref[...] = _ln(ALPHA * x + _dot(merged, wo_ref[...]), g_ref[...], b_ref[...])


def _merge(x2d, y_a, y_b, w_gate, b_gate, w_pa, w_pb, w_out, ln_g, ln_b):
    n = x2d.shape[0]
    tm = min(TOKEN_TILE, n)
    row = lambda width: pl.BlockSpec((tm, width), lambda i: (i, 0))
    full = lambda a: pl.BlockSpec(a.shape, lambda i: (0, 0))
    consts = [w_gate, b_gate, w_pa, w_pb, w_out, ln_g, ln_b]
    return pl.pallas_call(
        _merge_kernel,
        grid=(n // tm,),
        in_specs=[row(D_MODEL), row(W_A), row(W_B)] + [full(a) for a in consts],
        out_specs=row(D_MODEL),
        out_shape=jax.ShapeDtypeStruct((n, D_MODEL), F32),
        compiler_params=_params(("parallel",)),
        name=f"merge_n{n}",
    )(x2d, y_a, y_b, *consts)


def _pack_weights(w_in, b_in):
    offs = np.concatenate([[0], np.cumsum(SPLIT_SIZES)])
    names = ("q_a", "k_a", "v_a", "g_a", "q_b", "k_b", "v_b", "g_b", "q_i", "k_i", "w_i", "r_a", "r_b")
    w = {nm: w_in[:, :, offs[k]:offs[k + 1]] for k, nm in enumerate(names)}
    b = {nm: b_in[:, offs[k]:offs[k + 1]] for k, nm in enumerate(names)}
    pad = LANES - H_IDX

    def build(parts, last):
        order = (parts["q_a"] * SB_SCALE, parts["k_a"], parts["v_a"], parts["q_b"] * ATT_SCALE, parts["q_i"],
                 parts["k_b"], parts["k_b"], parts["v_b"], parts["v_b"], parts["k_i"], parts["k_i"], last)
        return jnp.concatenate(order, axis=-1)

    w_att = build(w, jnp.pad(w["w_i"], ((0, 0), (0, 0), (0, pad)))).astype(BF16)
    b_att = build(b, jnp.pad(b["w_i"], ((0, 0), (0, pad))))[:, None, :]
    w_gate = jnp.concatenate([w["g_a"], w["g_b"], w["r_a"], w["r_b"]], axis=-1).astype(BF16)
    b_gate = jnp.concatenate([b["g_a"], b["g_b"], b["r_a"], b["r_b"]], axis=-1)[:, None, :]
    return w_att, b_att, w_gate, b_gate


def _trunk_layer(x2d, batch, seq, past, n_select, weights, bias_tiles, bias_far):
    w_att, b_att, w_gate, b_gate, w_pa, w_pb, w_out, ln_g, ln_b = weights
    (q_a, k_a, v_a, q_b, q_i, kb2, k_b, vb2, v_b, ki2, k_i, w_i) = _project(x2d, w_att, b_att)
    shape3 = lambda a: a.reshape(batch, seq, a.shape[-1])
    if past is None:
        pk_a = pv_a = pk_b = pv_b = pk_i = None
    else:
        pk_a, pv_a, pk_b, pv_b, pk_i = past
        pk_a = pk_a.reshape(batch, -1, W_A)
        pv_a = pv_a.reshape(batch, -1, W_A)
    y_a = _stick_breaking(shape3(q_a), shape3(k_a), shape3(v_a), pk_a, pv_a)
    y_b = _sparse_attention(shape3(q_b), shape3(q_i), shape3(w_i), shape3(kb2), shape3(vb2), shape3(ki2),
                            pk_b, pv_b, pk_i, bias_tiles, bias_far, n_select)
    x_next = _merge(x2d, y_a.reshape(-1, W_A), y_b.reshape(-1, W_B),
                    w_gate, b_gate, w_pa, w_pb, w_out, ln_g, ln_b)
    return x_next, (shape3(k_a), shape3(v_a), shape3(k_b), shape3(v_b), shape3(k_i))


def kernel(x_prompt, x_sample, cache_sb_k, cache_sb_v, cache_dsa_k, cache_dsa_v, cache_idx_k,
           ln_in_g, ln_in_b, w_in, b_in, w_proj_a, w_proj_b, w_out, ln_g, ln_b, rel_bias):
    batch, seq, _ = x_prompt.shape
    dec_batch, dec_seq, _ = x_sample.shape
    past_len = cache_sb_k.shape[2]
    dec_pad = -(-dec_seq // ATT_BLOCK) * ATT_BLOCK
    n_sel_prompt = min(MAX_SELECT, seq // 4)
    n_sel_sample = min(MAX_SELECT, (past_len + dec_seq) // 4)
    w_att, b_att, w_gate, b_gate = _pack_weights(w_in, b_in)
    w_pa = w_proj_a.astype(BF16)
    w_pb = w_proj_b.astype(BF16)
    w_o = w_out.astype(BF16)
    bias_far = rel_bias[N_BUCKETS // 2 - 1]
    tiles = _bias_tiles(rel_bias, ATT_BLOCK)
    hp = _layer_norm(x_prompt.reshape(-1, D_MODEL), ln_in_g, ln_in_b)
    xs = jnp.pad(x_sample, ((0, 0), (0, dec_pad - dec_seq), (0, 0)))
    hs = _layer_norm(xs.reshape(-1, D_MODEL), ln_in_g, ln_in_b)
    rows_p, rows_s = [], []
    for layer in range(DEPTH):
        weights = (w_att[layer], b_att[layer], w_gate[layer], b_gate[layer], w_pa[layer], w_pb[layer], w_o[layer],
                   ln_g[layer].reshape(1, D_MODEL), ln_b[layer].reshape(1, D_MODEL))
        hp, new_p = _trunk_layer(hp, batch, seq, None, n_sel_prompt, weights, tiles, bias_far)
        past = (cache_sb_k[layer], cache_sb_v[layer], cache_dsa_k[layer], cache_dsa_v[layer], cache_idx_k[layer])
        hs, new_s = _trunk_layer(hs, dec_batch, dec_pad, past, n_sel_sample, weights, tiles, bias_far)
        rows_p.append(new_p)
        rows_s.append(tuple(r[:, :dec_seq] for r in new_s))

    def stack(rows, b, t):
        k_a, v_a, k_b, v_b, k_i = (jnp.stack(r) for r in zip(*rows))
        heads = lambda a: a.reshape(DEPTH, b, t, H_A, HD_A)
        return heads(k_a), heads(v_a), k_b, v_b, k_i

    y_s = hs.reshape(dec_batch, dec_pad, D_MODEL)[:, :dec_seq]
    return ((hp.reshape(batch, seq, D_MODEL), y_s) + stack(rows_p, batch, seq) + stack(rows_s, dec_batch, dec_seq))
```

```python
import functools
import math

import jax
import jax.numpy as jnp
import numpy as np
from jax import lax
from jax.experimental import pallas as pl
from jax.experimental.pallas import tpu as pltpu

F32 = jnp.float32
BF16 = jnp.bfloat16
I32 = jnp.int32
I16 = jnp.int16

D_MODEL = 1024
DEPTH = 4
CHUNK = 64
H_A = 8
HD_A = 64
W_A = H_A * HD_A
H_B = 8
HD_B = 64
W_B = H_B * HD_B
H_IDX = 8
D_IDX = 64
MAX_SELECT = 256
N_BUCKETS = 32
MAX_DISTANCE = 128
LN_EPS = 1e-5
ALPHA = (2 * DEPTH) ** 0.25
SB_SCALE = HD_A ** -0.5
ATT_SCALE = HD_B ** -0.5
SPLIT_SIZES = (W_A, W_A, W_A, W_A, W_B, HD_B, HD_B, W_B, H_IDX * D_IDX, D_IDX, H_IDX, D_MODEL, D_MODEL)

LANES = 128
SUBLANES = 8
PACKED_ROWS = 2 * SUBLANES
HALF = 64
NEG = -1e30
INT_MIN = -(2 ** 31)
I16_MIN = -(2 ** 15)
EXP_UNDERFLOW = -105.0
V7X_VMEM_LIMIT = 56 * 1024 * 1024
ATT_BLOCK = 256
TOKEN_TILE = 512

C_QA, C_KA, C_VA, C_QB, C_QI = 0, 512, 1024, 1536, 2048
C_KB2, C_VB2, C_KI2, C_WI = 2560, 2688, 2816, 2944
N_ATT = 3072
C_GA, C_GB, C_RA, C_RB = 0, 512, 1024, 2048
N_GATE = 3072


def _params(sem):
    return pltpu.CompilerParams(dimension_semantics=sem, vmem_limit_bytes=V7X_VMEM_LIMIT)


def _nt_dot(a, b):
    return lax.dot_general(a, b, (((1,), (1,)), ((), ())), preferred_element_type=F32)


def _dot(a, b):
    return jnp.dot(a, b, preferred_element_type=F32)


def _lane_lo():
    return lax.broadcasted_iota(I32, (1, LANES), 1) < HALF


def _ln(x, g, b):
    mu = jnp.mean(x, axis=-1, keepdims=True)
    xc = x - mu
    var = jnp.mean(xc * xc, axis=-1, keepdims=True)
    return xc * lax.rsqrt(var + LN_EPS) * g + b


def _ln_kernel(x_ref, g_ref, b_ref, o_ref):
    o_ref[...] = _ln(x_ref[...], g_ref[...], b_ref[...])


def _layer_norm(x2d, g, b):
    n = x2d.shape[0]
    tm = min(TOKEN_TILE, n)
    return pl.pallas_call(
        _ln_kernel,
        grid=(n // tm,),
        in_specs=[pl.BlockSpec((tm, D_MODEL), lambda i: (i, 0)),
                  pl.BlockSpec((1, D_MODEL), lambda i: (0, 0)),
                  pl.BlockSpec((1, D_MODEL), lambda i: (0, 0))],
        out_specs=pl.BlockSpec((tm, D_MODEL), lambda i: (i, 0)),
        out_shape=jax.ShapeDtypeStruct((n, D_MODEL), F32),
        compiler_params=_params(("parallel",)),
        name=f"ln_n{n}",
    )(x2d, g.reshape(1, D_MODEL), b.reshape(1, D_MODEL))


def _proj_kernel(x_ref, w_ref, b_ref, qa_ref, ka_ref, va_ref, qb_ref, qi_ref,
                 kb2_ref, kb_ref, vb2_ref, vb_ref, ki2_ref, ki_ref, wi_ref):
    xb = x_ref[...].astype(BF16)

    def seg(c0, n):
        return _dot(xb, w_ref[:, c0:c0 + n]) + b_ref[:, c0:c0 + n]

    qa_ref[...] = seg(C_QA, W_A).astype(BF16)
    ka_ref[...] = seg(C_KA, W_A)
    va_ref[...] = seg(C_VA, W_A)
    qb_ref[...] = seg(C_QB, W_B).astype(BF16)
    qi_ref[...] = seg(C_QI, H_IDX * D_IDX).astype(BF16)
    kb = seg(C_KB2, LANES)
    kb2_ref[...] = kb.astype(BF16)
    kb_ref[...] = kb[:, :HD_B]
    vb = seg(C_VB2, LANES)
    vb2_ref[...] = vb.astype(BF16)
    vb_ref[...] = vb[:, :HD_B]
    ki = seg(C_KI2, LANES)
    ki2_ref[...] = ki.astype(BF16)
    ki_ref[...] = ki[:, :D_IDX]
    wi_ref[...] = seg(C_WI, LANES)


def _project(x2d, w_att, b_att):
    n = x2d.shape[0]
    tm = min(TOKEN_TILE, n)
    row = lambda width: pl.BlockSpec((tm, width), lambda i: (i, 0))
    full = lambda a: pl.BlockSpec(a.shape, lambda i: (0, 0))
    sds = lambda width, dt: jax.ShapeDtypeStruct((n, width), dt)
    return pl.pallas_call(
        _proj_kernel,
        grid=(n // tm,),
        in_specs=[row(D_MODEL), full(w_att), full(b_att)],
        out_specs=[row(W_A), row(W_A), row(W_A), row(W_B), row(H_IDX * D_IDX),
                   row(LANES), row(HD_B), row(LANES), row(HD_B), row(LANES), row(D_IDX), row(LANES)],
        out_shape=[sds(W_A, BF16), sds(W_A, F32), sds(W_A, F32), sds(W_B, BF16), sds(H_IDX * D_IDX, BF16),
                   sds(LANES, BF16), sds(HD_B, F32), sds(LANES, BF16), sds(HD_B, F32),
                   sds(LANES, BF16), sds(D_IDX, F32), sds(LANES, F32)],
        compiler_params=_params(("parallel",)),
        name=f"proj_n{n}",
    )(x2d, w_att, b_att)


def _sb_kernel(*refs, tb, off, past_len, has_past):
    if has_past:
        q_ref, kn_ref, vn_ref, kp_ref, vp_ref, o_ref, kbf, vt, tri, acc_ref = refs
    else:
        q_ref, kn_ref, vn_ref, o_ref, kbf, vt, tri, acc_ref = refs
    i = pl.program_id(1)
    lo = _lane_lo()
    n_pairs = W_A // LANES

    @pl.when(i == 0)
    def _fill():
        r = lax.broadcasted_iota(I32, (LANES, LANES), 0)
        c = lax.broadcasted_iota(I32, (LANES, LANES), 1)
        eye = jnp.where(r == c, 1.0, 0.0).astype(BF16)

        def put(j0, k, v):
            n = k.shape[0]
            kbf[j0 * tb:j0 * tb + n, :] = k.astype(BF16)
            for jb in range(n // tb):
                for hp in range(n_pairs):
                    blk = v[jb * tb:(jb + 1) * tb, hp * LANES:(hp + 1) * LANES].astype(BF16)
                    vt[j0 + jb, hp * LANES:(hp + 1) * LANES, :] = _nt_dot(eye, blk).astype(BF16)

        if has_past:
            put(0, kp_ref[0], vp_ref[0])
        put(past_len // tb, kn_ref[0], vn_ref[0])
        rr = lax.broadcasted_iota(I32, (tb, tb), 0)
        cc = lax.broadcasted_iota(I32, (tb, tb), 1)
        tri[...] = jnp.where(cc > rr, 1.0, 0.0).astype(BF16)

    key_l = lax.broadcasted_iota(I32, (tb, tb), 0)
    query_l = lax.broadcasted_iota(I32, (tb, tb), 1)
    before = key_l < query_l
    row_lo = lax.broadcasted_iota(I32, (LANES, 1), 0) < HALF
    qblk = i + off

    def block(j, carries, diag):
        ks = pl.multiple_of(j * tb, tb)
        heads = range(H_A)
        zs = []
        for h in heads:
            hp, par = divmod(h, 2)
            q2 = q_ref[0, :, hp * LANES:(hp + 1) * LANES]
            zero_q = jnp.zeros_like(q2)
            qh = jnp.where(lo, q2, zero_q) if par == 0 else jnp.where(lo, zero_q, q2)
            zs.append(_nt_dot(kbf[pl.ds(ks, tb), hp * LANES:(hp + 1) * LANES], qh))
        lk_first, his, los, logsig = [], [], [], []
        for h in heads:
            z = zs[h]
            sp = jnp.maximum(z, 0.0) + jnp.log(1.0 + jnp.exp(-jnp.abs(z)))
            lk = -sp
            if diag:
                lk = jnp.where(before, lk, 0.0)
            hi = lk.astype(BF16)
            his.append(hi)
            los.append((lk - hi.astype(F32)).astype(BF16))
            lk_first.append(lk[0:1, :])
            logsig.append(z - sp)
        sufs = [_dot(tri[...], his[h]) + _dot(tri[...], los[h]) for h in heads]
        weights = []
        for h in heads:
            a = jnp.exp(logsig[h] + (sufs[h] + carries[h]))
            if diag:
                a = jnp.where(before, a, 0.0)
            weights.append(a.astype(BF16))
        for hp in range(n_pairs):
            vtb = vt[j, hp * LANES:(hp + 1) * LANES, :]
            acc_ref[hp] = acc_ref[hp] + jnp.where(row_lo, _dot(vtb, weights[2 * hp]), _dot(vtb, weights[2 * hp + 1]))
        return tuple(carries[h] + (sufs[h][0:1, :] + lk_first[h]) for h in heads)

    def any_weight_left(carries):
        top = carries[0]
        for c in carries[1:]:
            top = jnp.maximum(top, c)
        return (jnp.max(top) > EXP_UNDERFLOW).astype(I32)

    acc_ref[...] = jnp.zeros(acc_ref.shape, F32)
    carries = block(qblk, tuple(jnp.zeros((1, tb), F32) for _ in range(H_A)), True)

    def cond(state):
        return jnp.logical_and(state[0] < qblk, state[1] > 0)

    def body(state):
        t = state[0]
        carries = block(qblk - 1 - t, state[2:], False)
        return (t + 1, any_weight_left(carries)) + carries

    lax.while_loop(cond, body, (jnp.int32(0), any_weight_left(carries)) + carries)
    for hp in range(n_pairs):
        o_ref[0, :, hp * LANES:(hp + 1) * LANES] = acc_ref[hp].T


def _stick_breaking(q, k_new, v_new, k_past, v_past):
    b, t, _ = q.shape
    has_past = k_past is not None
    p = k_past.shape[1] if has_past else 0
    tb = ATT_BLOCK
    assert t % tb == 0 and p % tb == 0
    kv_new = pl.BlockSpec((1, t, W_A), lambda bi, i: (bi, 0, 0))
    kv_past = pl.BlockSpec((1, p, W_A), lambda bi, i: (bi, 0, 0))
    qo = pl.BlockSpec((1, tb, W_A), lambda bi, i: (bi, i, 0))
    ins = [q, k_new, v_new] + ([k_past, v_past] if has_past else [])
    specs = [qo, kv_new, kv_new] + ([kv_past, kv_past] if has_past else [])
    length = p + t
    return pl.pallas_call(
        functools.partial(_sb_kernel, tb=tb, off=p // tb, past_len=p, has_past=has_past),
        grid=(b, t // tb),
        in_specs=specs,
        out_specs=qo,
        out_shape=jax.ShapeDtypeStruct((b, t, W_A), F32),
        scratch_shapes=[pltpu.VMEM((length, W_A), BF16), pltpu.VMEM((length // tb, W_A, tb), BF16),
                        pltpu.VMEM((tb, tb), BF16), pltpu.VMEM((W_A // LANES, LANES, tb), F32)],
        compiler_params=_params(("parallel", "arbitrary")),
        name=f"sb_t{t}",
    )(*ins)


def _t5_bucket(rel):
    half = N_BUCKETS // 2
    max_exact = half // 2
    n = jnp.abs(rel)
    n_f = jnp.maximum(n, 1).astype(F32)
    large = max_exact + (jnp.log(n_f / max_exact) / math.log(MAX_DISTANCE / max_exact)
                         * (half - max_exact)).astype(I32)
    large = jnp.minimum(large, half - 1)
    return jnp.where(rel > 0, half, 0) + jnp.where(n < max_exact, n, large)


def _bias_kernel(rb_ref, o_ref, *, tb):
    h = pl.program_id(0)
    d = pl.program_id(1)
    key = lax.broadcasted_iota(I32, (tb, tb), 0)
    query = lax.broadcasted_iota(I32, (tb, tb), 1)
    bucket = _t5_bucket(key - query - d * tb)
    acc = jnp.zeros((tb, tb), F32)
    for bkt in range(N_BUCKETS):
        acc = jnp.where(bucket == bkt, rb_ref[bkt, h], acc)
    o_ref[0, 0] = acc


def _num_bias_diagonals(tb):
    return -(-MAX_DISTANCE // tb) + 1


def _bias_tiles(rel_bias, tb):
    nd = _num_bias_diagonals(tb)
    return pl.pallas_call(
        functools.partial(_bias_kernel, tb=tb),
        grid=(H_B, nd),
        in_specs=[pl.BlockSpec(memory_space=pltpu.SMEM)],
        out_specs=pl.BlockSpec((1, 1, tb, tb), lambda h, d: (h, d, 0, 0)),
        out_shape=jax.ShapeDtypeStruct((H_B, nd, tb, tb), F32),
        compiler_params=_params(("parallel", "parallel")),
        name=f"bias_tb{tb}",
    )(rel_bias)


def _dsa_kernel(*refs, tb, off, past_len, has_past, n_select, nd, length):
    if has_past:
        (qb_ref, qi_ref, wi_ref, kn_ref, vn_ref, in_ref, kp_ref, vp_ref, ip_ref, bt_ref, far_ref,
         o_ref, k2, vt_lo, vt_hi, i2, sc, sch, scl, mk, s_ref, m_ref, l_ref, acc_ref) = refs
    else:
        (qb_ref, qi_ref, wi_ref, kn_ref, vn_ref, in_ref, bt_ref, far_ref,
         o_ref, k2, vt_lo, vt_hi, i2, sc, sch, scl, mk, s_ref, m_ref, l_ref, acc_ref) = refs
    i = pl.program_id(1)
    lo = _lane_lo()
    qblk = i + off
    nkb = qblk + 1

    @pl.when(i == 0)
    def _fill():
        r = lax.broadcasted_iota(I32, (LANES, LANES), 0)
        c = lax.broadcasted_iota(I32, (LANES, LANES), 1)
        eye_lo = jnp.where((r == c) & (r < HALF), 1.0, 0.0).astype(BF16)
        eye_hi = jnp.where((r == c) & (r >= HALF), 1.0, 0.0).astype(BF16)

        def put(j0, kd, vd, idd):
            n = kd.shape[0]
            k2[j0 * tb:j0 * tb + n, :] = kd
            i2[j0 * tb:j0 * tb + n, :] = idd
            for jb in range(n // tb):
                blk = vd[jb * tb:(jb + 1) * tb, :]
                vt_lo[j0 + jb] = _nt_dot(eye_lo, blk).astype(BF16)
                vt_hi[j0 + jb] = _nt_dot(eye_hi, blk).astype(BF16)

        if has_past:
            rr = lax.broadcasted_iota(I32, (HALF, LANES), 0)
            cc = lax.broadcasted_iota(I32, (HALF, LANES), 1)
            dup = jnp.where((cc == rr) | (cc == rr + HALF), 1.0, 0.0).astype(BF16)
            widen = lambda x: _dot(x.astype(BF16), dup).astype(BF16)
            put(0, widen(kp_ref[0]), widen(vp_ref[0]), widen(ip_ref[0]))
        put(past_len // tb, kn_ref[0], vn_ref[0], in_ref[0])

    key_l = lax.broadcasted_iota(I32, (tb, tb), 0)
    query_l = lax.broadcasted_iota(I32, (tb, tb), 1)
    admissible = (key_l // CHUNK) <= (query_l // CHUNK)

    def head_queries(ref, hp):
        q2 = ref[0, :, hp * LANES:(hp + 1) * LANES]
        zero = jnp.zeros_like(q2)
        return jnp.where(lo, q2, zero), jnp.where(lo, zero, q2)

    wi_t = wi_ref[0].T

    def score_block(j, diag):
        ks = pl.multiple_of(j * tb, tb)
        kib = i2[pl.ds(ks, tb), :]
        dots = []
        for hp in range(H_IDX // 2):
            for qm in head_queries(qi_ref, hp):
                dots.append(_nt_dot(kib, qm))
        s = jnp.zeros((tb, tb), F32)
        for h in range(H_IDX):
            s = s + wi_t[h:h + 1, :] * jnp.maximum(dots[h], 0.0)
        bits = pltpu.bitcast(s, I32)
        key = bits ^ ((bits >> 31) & 0x7FFFFFFF)
        if diag:
            key = jnp.where(admissible, key, INT_MIN)
        sc[j] = key
        sch[j] = (key >> 16).astype(I16)
        scl[j] = ((key & 0xFFFF) + I16_MIN).astype(I16)

    def score_body(j, _):
        score_block(j, False)
        return 0

    lax.fori_loop(0, qblk, score_body, 0)
    score_block(qblk, True)

    kf = float(n_select)

    def count(pred_fn):
        def body(j, c):
            hit = pred_fn(sc[j], j * tb)
            ones = jnp.where(hit, 1.0, 0.0)
            return c + jnp.sum(ones.reshape(tb // SUBLANES, SUBLANES, tb), axis=0)
        c8 = lax.fori_loop(0, nkb, body, jnp.zeros((SUBLANES, tb), F32))
        return jnp.sum(c8, axis=0, keepdims=True)

    def count16(ref, pred_fn):
        def body(j, c):
            ones = jnp.where(pred_fn(ref[j]), jnp.ones((tb, tb), I16), jnp.zeros((tb, tb), I16))
            for g in range(tb // PACKED_ROWS):
                c = c + ones[g * PACKED_ROWS:(g + 1) * PACKED_ROWS, :]
            return c
        c16 = lax.fori_loop(0, nkb, body, jnp.zeros((PACKED_ROWS, tb), I16))
        return jnp.sum(c16.astype(F32), axis=0, keepdims=True)

    def search16(ref, base):
        def bit_body(t, tau):
            cand = tau + jnp.left_shift(jnp.int32(1), 15 - t)
            cand16 = cand.astype(I16)
            c = base + count16(ref, lambda blk: blk >= cand16)
            return jnp.where(c >= kf, cand, tau)
        return lax.fori_loop(0, 16, bit_body, jnp.full((1, tb), I16_MIN, I32))

    tau_hi = search16(sch, 0.0)
    tau_hi16 = tau_hi.astype(I16)
    above = count16(sch, lambda blk: blk > tau_hi16)

    def narrow(j, _):
        scl[j] = jnp.where(sch[j] == tau_hi16, scl[j], jnp.full((tb, tb), I16_MIN, I16))
        return 0

    lax.fori_loop(0, nkb, narrow, 0)
    tau_lo = search16(scl, above)
    tau = tau_hi * 65536 + (tau_lo - I16_MIN)
    cnt_ge = count(lambda blk, ks: blk >= tau)
    has_thr = tau > INT_MIN
    tie = has_thr & (cnt_ge > kf)
    any_tie = jnp.max(jnp.where(tie, 1.0, 0.0)) > 0.0

    def write_mask(sel_fn):
        def body(j, _):
            mk[j] = jnp.where(sel_fn(sc[j], j * tb), 0.0, NEG)
            return 0
        lax.fori_loop(0, nkb, body, 0)

    @pl.when(jnp.logical_not(any_tie))
    def _plain():
        thr = jnp.where(has_thr, tau, INT_MIN + 1)
        write_mask(lambda blk, ks: blk >= thr)

    @pl.when(any_tie)
    def _ties():
        need = kf - count(lambda blk, ks: blk > tau)
        q_idx = jnp.zeros((1, tb), I32)
        for bit in reversed(range(max(1, (length - 1).bit_length()))):
            cand = q_idx + (1 << bit)
            c = count(lambda blk, ks: (blk == tau) & ((key_l + ks) < cand))
            q_idx = jnp.where(c < need, cand, q_idx)
        last_eq = jnp.where(has_thr, jnp.where(tie, q_idx, length), -1)
        write_mask(lambda blk, ks: (blk > tau) | ((blk == tau) & ((key_l + ks) <= last_eq)))

    m_ref[...] = jnp.full(m_ref.shape, NEG, F32)
    l_ref[...] = jnp.zeros(l_ref.shape, F32)
    acc_ref[...] = jnp.zeros(acc_ref.shape, F32)
    row_lo = lax.broadcasted_iota(I32, (LANES, 1), 0) < HALF
    n_far = jnp.maximum(qblk - nd + 1, 0)

    def logits(j, d):
        ks = pl.multiple_of(j * tb, tb)
        kblk = k2[pl.ds(ks, tb), :]
        mblk = mk[j]
        for hp in range(H_B // 2):
            for par, qm in enumerate(head_queries(qb_ref, hp)):
                h = 2 * hp + par
                s = _nt_dot(kblk, qm) + mblk
                if d is not None:
                    s = s + bt_ref[h, d]
                s_ref[h, j] = s
                top = jnp.max(s, axis=0, keepdims=True)
                if d is None:
                    top = top + far_ref[h]
                m_ref[h] = jnp.maximum(m_ref[h], top)

    def weigh(j, far):
        ps = []
        for h in range(H_B):
            shift = m_ref[h] - far_ref[h] if far else m_ref[h]
            p = jnp.exp(s_ref[h, j] - shift)
            l_ref[h] = l_ref[h] + jnp.sum(p, axis=0, keepdims=True)
            ps.append(p.astype(BF16))
        for hp in range(H_B // 2):
            acc_ref[hp] = acc_ref[hp] + (_dot(vt_lo[j], ps[2 * hp]) + _dot(vt_hi[j], ps[2 * hp + 1]))

    def far_logits(j, _):
        logits(j, None)
        return 0

    lax.fori_loop(0, n_far, far_logits, 0)
    for d in range(nd):
        @pl.when(qblk - d >= 0)
        def _near_logits(d=d):
            logits(qblk - d, d)

    def far_weigh(j, _):
        weigh(j, True)
        return 0

    lax.fori_loop(0, n_far, far_weigh, 0)
    for d in range(nd):
        @pl.when(qblk - d >= 0)
        def _near_weigh(d=d):
            weigh(qblk - d, False)

    for hp in range(H_B // 2):
        denom = jnp.where(row_lo, l_ref[2 * hp], l_ref[2 * hp + 1])
        o_ref[0, :, hp * LANES:(hp + 1) * LANES] = (acc_ref[hp] / denom).T


def _sparse_attention(q_b, q_i, w_i, k2_new, v2_new, i2_new, k_past, v_past, i_past, bias_tiles, bias_far,
                      n_select):
    b, t, _ = q_b.shape
    has_past = k_past is not None
    p = k_past.shape[1] if has_past else 0
    tb = ATT_BLOCK
    assert t % tb == 0 and p % tb == 0 and tb % CHUNK == 0 and bias_tiles.shape[2] == tb
    length = p + t
    nd = bias_tiles.shape[1]
    nb = length // tb
    qspec = pl.BlockSpec((1, tb, W_B), lambda bi, i: (bi, i, 0))
    new = pl.BlockSpec((1, t, LANES), lambda bi, i: (bi, 0, 0))
    past = pl.BlockSpec((1, p, HD_B), lambda bi, i: (bi, 0, 0))
    ins = [q_b, q_i, w_i, k2_new, v2_new, i2_new] + ([k_past, v_past, i_past] if has_past else [])
    specs = ([qspec, qspec, pl.BlockSpec((1, tb, LANES), lambda bi, i: (bi, i, 0)), new, new, new]
             + ([past, past, past] if has_past else []))
    ins += [bias_tiles, bias_far]
    specs += [pl.BlockSpec(bias_tiles.shape, lambda bi, i: (0, 0, 0, 0)), pl.BlockSpec(memory_space=pltpu.SMEM)]
    return pl.pallas_call(
        functools.partial(_dsa_kernel, tb=tb, off=p // tb, past_len=p, has_past=has_past,
                          n_select=n_select, nd=nd, length=length),
        grid=(b, t // tb),
        in_specs=specs,
        out_specs=qspec,
        out_shape=jax.ShapeDtypeStruct((b, t, W_B), F32),
        scratch_shapes=[pltpu.VMEM((length, LANES), BF16),
                        pltpu.VMEM((nb, LANES, tb), BF16), pltpu.VMEM((nb, LANES, tb), BF16),
                        pltpu.VMEM((length, LANES), BF16),
                        pltpu.VMEM((nb, tb, tb), I32),
                        pltpu.VMEM((nb, tb, tb), I16), pltpu.VMEM((nb, tb, tb), I16),
                        pltpu.VMEM((nb, tb, tb), F32),
                        pltpu.VMEM((H_B, nb, tb, tb), F32),
                        pltpu.VMEM((H_B, 1, tb), F32), pltpu.VMEM((H_B, 1, tb), F32),
                        pltpu.VMEM((H_B // 2, LANES, tb), F32)],
        compiler_params=_params(("parallel", "arbitrary")),
        name=f"dsa_t{t}",
    )(*ins)


def _sigmoid(x):
    return 1.0 / (1.0 + jnp.exp(-x))


def _merge_kernel(x_ref, ya_ref, yb_ref, wg_ref, bg_ref, wpa_ref, wpb_ref, wo_ref, g_ref, b_ref, o_ref):
    x = x_ref[...]
    xb = x.astype(BF16)

    def seg(c0, n):
        return _dot(xb, wg_ref[:, c0:c0 + n]) + bg_ref[:, c0:c0 + n]

    g_a = seg(C_GA, W_A)
    y_a = (ya_ref[...] * (g_a * _sigmoid(g_a))).astype(BF16)
    branch_a = _sigmoid(seg(C_RA, D_MODEL)) * _dot(y_a, wpa_ref[...])
    g_b = seg(C_GB, W_B)
    y_b = (yb_ref[...] * (g_b * _sigmoid(g_b))).astype(BF16)
    branch_b = _sigmoid(seg(C_RB, D_MODEL)) * _dot(y_b, wpb_ref[...])
    merged = (branch_a + branch_b).astype(BF16)
    o_ref[...] = _ln(ALPHA * x + _dot(merged, wo_ref[...]), g_ref[...], b_ref[...])


def _merge(x2d, y_a, y_b, w_gate, b_gate, w_pa, w_pb, w_out, ln_g, ln_b):
    n = x2d.shape[0]
    tm = min(TOKEN_TILE, n)
    row = lambda width: pl.BlockSpec((tm, width), lambda i: (i, 0))
    full = lambda a: pl.BlockSpec(a.shape, lambda i: (0, 0))
    consts = [w_gate, b_gate, w_pa, w_pb, w_out, ln_g, ln_b]
    return pl.pallas_call(
        _merge_kernel,
        grid=(n // tm,),
        in_specs=[row(D_MODEL), row(W_A), row(W_B)] + [full(a) for a in consts],
        out_specs=row(D_MODEL),
        out_shape=jax.ShapeDtypeStruct((n, D_MODEL), F32),
        compiler_params=_params(("parallel",)),
        name=f"merge_n{n}",
    )(x2d, y_a, y_b, *consts)


def _pack_weights(w_in, b_in):
    offs = np.concatenate([[0], np.cumsum(SPLIT_SIZES)])
    names = ("q_a", "k_a", "v_a", "g_a", "q_b", "k_b", "v_b", "g_b", "q_i", "k_i", "w_i", "r_a", "r_b")
    w = {nm: w_in[:, :, offs[k]:offs[k + 1]] for k, nm in enumerate(names)}
    b = {nm: b_in[:, offs[k]:offs[k + 1]] for k, nm in enumerate(names)}
    pad = LANES - H_IDX

    def build(parts, last):
        order = (parts["q_a"] * SB_SCALE, parts["k_a"], parts["v_a"], parts["q_b"] * ATT_SCALE, parts["q_i"],
                 parts["k_b"], parts["k_b"], parts["v_b"], parts["v_b"], parts["k_i"], parts["k_i"], last)
        return jnp.concatenate(order, axis=-1)

    w_att = build(w, jnp.pad(w["w_i"], ((0, 0), (0, 0), (0, pad)))).astype(BF16)
    b_att = build(b, jnp.pad(b["w_i"], ((0, 0), (0, pad))))[:, None, :]
    w_gate = jnp.concatenate([w["g_a"], w["g_b"], w["r_a"], w["r_b"]], axis=-1).astype(BF16)
    b_gate = jnp.concatenate([b["g_a"], b["g_b"], b["r_a"], b["r_b"]], axis=-1)[:, None, :]
    return w_att, b_att, w_gate, b_gate


def _trunk_layer(x2d, batch, seq, past, n_select, weights, bias_tiles, bias_far):
    w_att, b_att, w_gate, b_gate, w_pa, w_pb, w_out, ln_g, ln_b = weights
    (q_a, k_a, v_a, q_b, q_i, kb2, k_b, vb2, v_b, ki2, k_i, w_i) = _project(x2d, w_att, b_att)
    shape3 = lambda a: a.reshape(batch, seq, a.shape[-1])
    if past is None:
        pk_a = pv_a = pk_b = pv_b = pk_i = None
    else:
        pk_a, pv_a, pk_b, pv_b, pk_i = past
        pk_a = pk_a.reshape(batch, -1, W_A)
        pv_a = pv_a.reshape(batch, -1, W_A)
    y_a = _stick_breaking(shape3(q_a), shape3(k_a), shape3(v_a), pk_a, pv_a)
    y_b = _sparse_attention(shape3(q_b), shape3(q_i), shape3(w_i), shape3(kb2), shape3(vb2), shape3(ki2),
                            pk_b, pv_b, pk_i, bias_tiles, bias_far, n_select)
    x_next = _merge(x2d, y_a.reshape(-1, W_A), y_b.reshape(-1, W_B),
                    w_gate, b_gate, w_pa, w_pb, w_out, ln_g, ln_b)
    return x_next, (shape3(k_a), shape3(v_a), shape3(k_b), shape3(v_b), shape3(k_i))


def kernel(x_prompt, x_sample, cache_sb_k, cache_sb_v, cache_dsa_k, cache_dsa_v, cache_idx_k,
           ln_in_g, ln_in_b, w_in, b_in, w_proj_a, w_proj_b, w_out, ln_g, ln_b, rel_bias):
    batch, seq, _ = x_prompt.shape
    dec_batch, dec_seq, _ = x_sample.shape
    past_len = cache_sb_k.shape[2]
    dec_pad = -(-dec_seq // ATT_BLOCK) * ATT_BLOCK
    n_sel_prompt = min(MAX_SELECT, seq // 4)
    n_sel_sample = min(MAX_SELECT, (past_len + dec_seq) // 4)
    w_att, b_att, w_gate, b_gate = _pack_weights(w_in, b_in)
    w_pa = w_proj_a.astype(BF16)
    w_pb = w_proj_b.astype(BF16)
    w_o = w_out.astype(BF16)
    bias_far = rel_bias[N_BUCKETS // 2 - 1]
    tiles = _bias_tiles(rel_bias, ATT_BLOCK)
    hp = _layer_norm(x_prompt.reshape(-1, D_MODEL), ln_in_g, ln_in_b)
    xs = jnp.pad(x_sample, ((0, 0), (0, dec_pad - dec_seq), (0, 0)))
    hs = _layer_norm(xs.reshape(-1, D_MODEL), ln_in_g, ln_in_b)
    rows_p, rows_s = [], []
    for layer in range(DEPTH):
        weights = (w_att[layer], b_att[layer], w_gate[layer], b_gate[layer], w_pa[layer], w_pb[layer], w_o[layer],
                   ln_g[layer].reshape(1, D_MODEL), ln_b[layer].reshape(1, D_MODEL))
        hp, new_p = _trunk_layer(hp, batch, seq, None, n_sel_prompt, weights, tiles, bias_far)
        past = (cache_sb_k[layer], cache_sb_v[layer], cache_dsa_k[layer], cache_dsa_v[layer], cache_idx_k[layer])
        hs, new_s = _trunk_layer(hs, dec_batch, dec_pad, past, n_sel_sample, weights, tiles, bias_far)
        rows_p.append(new_p)
        rows_s.append(tuple(r[:, :dec_seq] for r in new_s))

    def stack(rows, b, t):
        k_a, v_a, k_b, v_b, k_i = (jnp.stack(r) for r in zip(*rows))
        heads = lambda a: a.reshape(DEPTH, b, t, H_A, HD_A)
        return heads(k_a), heads(v_a), k_b, v_b, k_i

    y_s = hs.reshape(dec_batch, dec_pad, D_MODEL)[:, :dec_seq]
    return ((hp.reshape(batch, seq, D_MODEL), y_s) + stack(rows_p, batch, seq) + stack(rows_s, dec_batch, dec_seq))
```

```python
import functools
import math

import jax
import jax.numpy as jnp
import numpy as np
from jax import lax
from jax.experimental import pallas as pl
from jax.experimental.pallas import tpu as pltpu

F32 = jnp.float32
BF16 = jnp.bfloat16
I32 = jnp.int32
I16 = jnp.int16

D_MODEL = 1024
DEPTH = 4
CHUNK = 64
H_A = 8
HD_A = 64
W_A = H_A * HD_A
H_B = 8
HD_B = 64
W_B = H_B * HD_B
H_IDX = 8
D_IDX = 64
MAX_SELECT = 256
N_BUCKETS = 32
MAX_DISTANCE = 128
LN_EPS = 1e-5
ALPHA = (2 * DEPTH) ** 0.25
SB_SCALE = HD_A ** -0.5
ATT_SCALE = HD_B ** -0.5
SPLIT_SIZES = (W_A, W_A, W_A, W_A, W_B, HD_B, HD_B, W_B, H_IDX * D_IDX, D_IDX, H_IDX, D_MODEL, D_MODEL)

LANES = 128
SUBLANES = 8
PACKED_ROWS = 2 * SUBLANES
HALF = 64
NEG = -1e30
INT_MIN = -(2 ** 31)
I16_MIN = -(2 ** 15)
EXP_UNDERFLOW = -105.0
V7X_VMEM_LIMIT = 56 * 1024 * 1024
ATT_BLOCK = 256
TOKEN_TILE = 1024

C_QA, C_KA, C_VA, C_QB, C_QI = 0, 512, 1024, 1536, 2048
C_KB2, C_VB2, C_KI2, C_WI = 2560, 2688, 2816, 2944
N_ATT = 3072
C_GA, C_GB, C_RA, C_RB = 0, 512, 1024, 2048
N_GATE = 3072


def _params(sem):
    return pltpu.CompilerParams(dimension_semantics=sem, vmem_limit_bytes=V7X_VMEM_LIMIT)


def _nt_dot(a, b):
    return lax.dot_general(a, b, (((1,), (1,)), ((), ())), preferred_element_type=F32)


def _dot(a, b):
    return jnp.dot(a, b, preferred_element_type=F32)


def _lane_lo():
    return lax.broadcasted_iota(I32, (1, LANES), 1) < HALF


def _ln(x, g, b):
    mu = jnp.mean(x, axis=-1, keepdims=True)
    xc = x - mu
    var = jnp.mean(xc * xc, axis=-1, keepdims=True)
    return xc * lax.rsqrt(var + LN_EPS) * g + b


def _ln_kernel(x_ref, g_ref, b_ref, o_ref):
    o_ref[...] = _ln(x_ref[...], g_ref[...], b_ref[...])


def _layer_norm(x2d, g, b):
    n = x2d.shape[0]
    tm = min(TOKEN_TILE, n)
    return pl.pallas_call(
        _ln_kernel,
        grid=(n // tm,),
        in_specs=[pl.BlockSpec((tm, D_MODEL), lambda i: (i, 0)),
                  pl.BlockSpec((1, D_MODEL), lambda i: (0, 0)),
                  pl.BlockSpec((1, D_MODEL), lambda i: (0, 0))],
        out_specs=pl.BlockSpec((tm, D_MODEL), lambda i: (i, 0)),
        out_shape=jax.ShapeDtypeStruct((n, D_MODEL), F32),
        compiler_params=_params(("parallel",)),
        name=f"ln_n{n}",
    )(x2d, g.reshape(1, D_MODEL), b.reshape(1, D_MODEL))


def _proj_kernel(x_ref, w_ref, b_ref, ka_all, va_all, kb_all, vb_all, ki_all,
                 qa_ref, qb_ref, qi_ref, kb2_ref, vb2_ref, ki2_ref, wi_ref,
                 ka_ref, va_ref, kb_ref, vb_ref, ki_ref):
    del ka_all, va_all, kb_all, vb_all, ki_all
    xb = x_ref[...].astype(BF16)

    def seg(c0, n):
        return _dot(xb, w_ref[:, c0:c0 + n]) + b_ref[:, c0:c0 + n]

    qa_ref[...] = seg(C_QA, W_A).astype(BF16)
    ka_ref[...] = seg(C_KA, W_A)
    va_ref[...] = seg(C_VA, W_A)
    qb_ref[...] = seg(C_QB, W_B).astype(BF16)
    qi_ref[...] = seg(C_QI, H_IDX * D_IDX).astype(BF16)
    kb = seg(C_KB2, LANES)
    kb2_ref[...] = kb.astype(BF16)
    kb_ref[...] = kb[:, :HD_B]
    vb = seg(C_VB2, LANES)
    vb2_ref[...] = vb.astype(BF16)
    vb_ref[...] = vb[:, :HD_B]
    ki = seg(C_KI2, LANES)
    ki2_ref[...] = ki.astype(BF16)
    ki_ref[...] = ki[:, :D_IDX]
    wi_ref[...] = seg(C_WI, LANES)


def _project(x2d, w_att, b_att, new_rows, layer):
    n = x2d.shape[0]
    tm = min(TOKEN_TILE, n)
    row = lambda width: pl.BlockSpec((tm, width), lambda i: (i, 0))
    full = lambda a: pl.BlockSpec(a.shape, lambda i: (0, 0))
    sds = lambda width, dt: jax.ShapeDtypeStruct((n, width), dt)
    slab = lambda a: pl.BlockSpec((None, tm, a.shape[-1]), lambda i: (layer, i, 0))
    n_plain = 7
    outs = pl.pallas_call(
        _proj_kernel,
        grid=(n // tm,),
        in_specs=[row(D_MODEL), full(w_att), full(b_att)] + [pl.BlockSpec(memory_space=pl.ANY)] * len(new_rows),
        out_specs=[row(W_A), row(W_B), row(H_IDX * D_IDX), row(LANES), row(LANES), row(LANES), row(LANES)]
                  + [slab(a) for a in new_rows],
        out_shape=[sds(W_A, BF16), sds(W_B, BF16), sds(H_IDX * D_IDX, BF16),
                   sds(LANES, BF16), sds(LANES, BF16), sds(LANES, BF16), sds(LANES, F32)]
                  + [jax.ShapeDtypeStruct(a.shape, a.dtype) for a in new_rows],
        input_output_aliases={3 + k: n_plain + k for k in range(len(new_rows))},
        compiler_params=_params(("parallel",)),
        name=f"proj_n{n}",
    )(x2d, w_att, b_att, *new_rows)
    return outs[:n_plain], tuple(outs[n_plain:])


def _sb_kernel(*refs, tb, off, past_len, has_past):
    if has_past:
        q_ref, kn_ref, vn_ref, kp_ref, vp_ref, o_ref, kbf, vt, tri, acc_ref = refs
    else:
        q_ref, kn_ref, vn_ref, o_ref, kbf, vt, tri, acc_ref = refs
    i = pl.program_id(1)
    lo = _lane_lo()
    n_pairs = W_A // LANES

    @pl.when(i == 0)
    def _fill():
        r = lax.broadcasted_iota(I32, (LANES, LANES), 0)
        c = lax.broadcasted_iota(I32, (LANES, LANES), 1)
        eye = jnp.where(r == c, 1.0, 0.0).astype(BF16)

        def put(j0, k, v):
            n = k.shape[0]
            kbf[j0 * tb:j0 * tb + n, :] = k.astype(BF16)
            for jb in range(n // tb):
                for hp in range(n_pairs):
                    blk = v[jb * tb:(jb + 1) * tb, hp * LANES:(hp + 1) * LANES].astype(BF16)
                    vt[j0 + jb, hp * LANES:(hp + 1) * LANES, :] = _nt_dot(eye, blk).astype(BF16)

        if has_past:
            put(0, kp_ref[0], vp_ref[0])
        put(past_len // tb, kn_ref[0], vn_ref[0])
        rr = lax.broadcasted_iota(I32, (tb, tb), 0)
        cc = lax.broadcasted_iota(I32, (tb, tb), 1)
        tri[...] = jnp.where(cc > rr, 1.0, 0.0).astype(BF16)

    key_l = lax.broadcasted_iota(I32, (tb, tb), 0)
    query_l = lax.broadcasted_iota(I32, (tb, tb), 1)
    before = key_l < query_l
    row_lo = lax.broadcasted_iota(I32, (LANES, 1), 0) < HALF
    qblk = i + off

    def block(j, carries, diag):
        ks = pl.multiple_of(j * tb, tb)
        heads = range(H_A)
        zs = []
        for h in heads:
            hp, par = divmod(h, 2)
            q2 = q_ref[0, :, hp * LANES:(hp + 1) * LANES]
            zero_q = jnp.zeros_like(q2)
            qh = jnp.where(lo, q2, zero_q) if par == 0 else jnp.where(lo, zero_q, q2)
            zs.append(_nt_dot(kbf[pl.ds(ks, tb), hp * LANES:(hp + 1) * LANES], qh))
        sp_first, his, los, logsig = [], [], [], []
        for h in heads:
            z = zs[h]
            sp = jnp.maximum(z, 0.0) + jnp.log(1.0 + jnp.exp(-jnp.abs(z)))
            logsig.append(z - sp)
            if diag:
                sp = jnp.where(before, sp, 0.0)
            hi = sp.astype(BF16)
            his.append(hi)
            los.append((sp - hi.astype(F32)).astype(BF16))
            sp_first.append(sp[0:1, :])
        sufs = [_dot(tri[...], his[h]) + _dot(tri[...], los[h]) for h in heads]
        weights = []
        for h in heads:
            a = jnp.exp(logsig[h] + (carries[h] - sufs[h]))
            if diag:
                a = jnp.where(before, a, 0.0)
            weights.append(a.astype(BF16))
        for hp in range(n_pairs):
            vtb = vt[j, hp * LANES:(hp + 1) * LANES, :]
            acc_ref[hp] = acc_ref[hp] + jnp.where(row_lo, _dot(vtb, weights[2 * hp]), _dot(vtb, weights[2 * hp + 1]))
        return tuple(carries[h] - (sufs[h][0:1, :] + sp_first[h]) for h in heads)

    def any_weight_left(carries):
        top = carries[0]
        for c in carries[1:]:
            top = jnp.maximum(top, c)
        return (jnp.max(top) > EXP_UNDERFLOW).astype(I32)

    acc_ref[...] = jnp.zeros(acc_ref.shape, F32)
    carries = block(qblk, tuple(jnp.zeros((1, tb), F32) for _ in range(H_A)), True)

    def cond(state):
        return jnp.logical_and(state[0] < qblk, state[1] > 0)

    def body(state):
        t = state[0]
        carries = block(qblk - 1 - t, state[2:], False)
        return (t + 1, any_weight_left(carries)) + carries

    lax.while_loop(cond, body, (jnp.int32(0), any_weight_left(carries)) + carries)
    for hp in range(n_pairs):
        o_ref[0, :, hp * LANES:(hp + 1) * LANES] = acc_ref[hp].T


def _stick_breaking(q, k_new, v_new, layer, k_past, v_past):
    b, t, _ = q.shape
    has_past = k_past is not None
    p = k_past.shape[1] if has_past else 0
    tb = ATT_BLOCK
    assert t % tb == 0 and p % tb == 0
    kv_new = pl.BlockSpec((None, 1, t, W_A), lambda bi, i: (layer, bi, 0, 0))
    kv_past = pl.BlockSpec((1, p, W_A), lambda bi, i: (bi, 0, 0))
    qo = pl.BlockSpec((1, tb, W_A), lambda bi, i: (bi, i, 0))
    ins = [q, k_new, v_new] + ([k_past, v_past] if has_past else [])
    specs = [qo, kv_new, kv_new] + ([kv_past, kv_past] if has_past else [])
    length = p + t
    return pl.pallas_call(
        functools.partial(_sb_kernel, tb=tb, off=p // tb, past_len=p, has_past=has_past),
        grid=(b, t // tb),
        in_specs=specs,
        out_specs=qo,
        out_shape=jax.ShapeDtypeStruct((b, t, W_A), F32),
        scratch_shapes=[pltpu.VMEM((length, W_A), BF16), pltpu.VMEM((length // tb, W_A, tb), BF16),
                        pltpu.VMEM((tb, tb), BF16), pltpu.VMEM((W_A // LANES, LANES, tb), F32)],
        compiler_params=_params(("parallel", "arbitrary")),
        name=f"sb_t{t}",
    )(*ins)


def _t5_bucket(rel):
    half = N_BUCKETS // 2
    max_exact = half // 2
    n = jnp.abs(rel)
    n_f = jnp.maximum(n, 1).astype(F32)
    large = max_exact + (jnp.log(n_f / max_exact) / math.log(MAX_DISTANCE / max_exact)
                         * (half - max_exact)).astype(I32)
    large = jnp.minimum(large, half - 1)
    return jnp.where(rel > 0, half, 0) + jnp.where(n < max_exact, n, large)


def _bias_kernel(rb_ref, o_ref, *, tb):
    h = pl.program_id(0)
    d = pl.program_id(1)
    key = lax.broadcasted_iota(I32, (tb, tb), 0)
    query = lax.broadcasted_iota(I32, (tb, tb), 1)
    bucket = _t5_bucket(key - query - d * tb)
    acc = jnp.zeros((tb, tb), F32)
    for bkt in range(N_BUCKETS):
        acc = jnp.where(bucket == bkt, rb_ref[bkt, h], acc)
    o_ref[0, 0] = acc


def _num_bias_diagonals(tb):
    return -(-MAX_DISTANCE // tb) + 1


def _bias_tiles(rel_bias, tb):
    nd = _num_bias_diagonals(tb)
    return pl.pallas_call(
        functools.partial(_bias_kernel, tb=tb),
        grid=(H_B, nd),
        in_specs=[pl.BlockSpec(memory_space=pltpu.SMEM)],
        out_specs=pl.BlockSpec((1, 1, tb, tb), lambda h, d: (h, d, 0, 0)),
        out_shape=jax.ShapeDtypeStruct((H_B, nd, tb, tb), F32),
        compiler_params=_params(("parallel", "parallel")),
        name=f"bias_tb{tb}",
    )(rel_bias)


def _dsa_kernel(*refs, tb, off, past_len, has_past, n_select, nd, length):
    if has_past:
        (qb_ref, qi_ref, wi_ref, kn_ref, vn_ref, in_ref, kp_ref, vp_ref, ip_ref, bt_ref, far_ref,
         o_ref, k2, vt_lo, vt_hi, i2, sc, sch, scl, mk, s_ref, m_ref, l_ref, acc_ref) = refs
    else:
        (qb_ref, qi_ref, wi_ref, kn_ref, vn_ref, in_ref, bt_ref, far_ref,
         o_ref, k2, vt_lo, vt_hi, i2, sc, sch, scl, mk, s_ref, m_ref, l_ref, acc_ref) = refs
    i = pl.program_id(1)
    lo = _lane_lo()
    qblk = i + off
    nkb = qblk + 1

    @pl.when(i == 0)
    def _fill():
        r = lax.broadcasted_iota(I32, (LANES, LANES), 0)
        c = lax.broadcasted_iota(I32, (LANES, LANES), 1)
        eye_lo = jnp.where((r == c) & (r < HALF), 1.0, 0.0).astype(BF16)
        eye_hi = jnp.where((r == c) & (r >= HALF), 1.0, 0.0).astype(BF16)

        def put(j0, kd, vd, idd):
            n = kd.shape[0]
            k2[j0 * tb:j0 * tb + n, :] = kd
            i2[j0 * tb:j0 * tb + n, :] = idd
            for jb in range(n // tb):
                blk = vd[jb * tb:(jb + 1) * tb, :]
                vt_lo[j0 + jb] = _nt_dot(eye_lo, blk).astype(BF16)
                vt_hi[j0 + jb] = _nt_dot(eye_hi, blk).astype(BF16)

        if has_past:
            rr = lax.broadcasted_iota(I32, (HALF, LANES), 0)
            cc = lax.broadcasted_iota(I32, (HALF, LANES), 1)
            dup = jnp.where((cc == rr) | (cc == rr + HALF), 1.0, 0.0).astype(BF16)
            widen = lambda x: _dot(x.astype(BF16), dup).astype(BF16)
            put(0, widen(kp_ref[0]), widen(vp_ref[0]), widen(ip_ref[0]))
        put(past_len // tb, kn_ref[0], vn_ref[0], in_ref[0])

    key_l = lax.broadcasted_iota(I32, (tb, tb), 0)
    query_l = lax.broadcasted_iota(I32, (tb, tb), 1)
    admissible = (key_l // CHUNK) <= (query_l // CHUNK)

    def head_queries(ref, hp):
        q2 = ref[0, :, hp * LANES:(hp + 1) * LANES]
        zero = jnp.zeros_like(q2)
        return jnp.where(lo, q2, zero), jnp.where(lo, zero, q2)

    wi_t = wi_ref[0].T

    def score_block(j, diag):
        ks = pl.multiple_of(j * tb, tb)
        kib = i2[pl.ds(ks, tb), :]
        dots = []
        for hp in range(H_IDX // 2):
            for qm in head_queries(qi_ref, hp):
                dots.append(_nt_dot(kib, qm))
        s = jnp.zeros((tb, tb), F32)
        for h in range(H_IDX):
            s = s + wi_t[h:h + 1, :] * jnp.maximum(dots[h], 0.0)
        bits = pltpu.bitcast(s, I32)
        key = bits ^ ((bits >> 31) & 0x7FFFFFFF)
        if diag:
            key = jnp.where(admissible, key, INT_MIN)
        sc[j] = key
        sch[j] = (key >> 16).astype(I16)
        scl[j] = ((key & 0xFFFF) + I16_MIN).astype(I16)

    def score_body(j, _):
        score_block(j, False)
        return 0

    lax.fori_loop(0, qblk, score_body, 0)
    score_block(qblk, True)

    kf = float(n_select)

    def count(pred_fn):
        def body(j, c):
            hit = pred_fn(sc[j], j * tb)
            ones = jnp.where(hit, 1.0, 0.0)
            return c + jnp.sum(ones.reshape(tb // SUBLANES, SUBLANES, tb), axis=0)
        c8 = lax.fori_loop(0, nkb, body, jnp.zeros((SUBLANES, tb), F32))
        return jnp.sum(c8, axis=0, keepdims=True)

    def threshold_of(n_blocks):
        def count16(ref, pred_fn):
            c = jnp.zeros((PACKED_ROWS, tb), I16)
            for j in range(n_blocks):
                ones = jnp.where(pred_fn(ref[j]), jnp.ones((tb, tb), I16), jnp.zeros((tb, tb), I16))
                for g in range(tb // PACKED_ROWS):
                    c = c + ones[g * PACKED_ROWS:(g + 1) * PACKED_ROWS, :]
            return jnp.sum(c.astype(F32), axis=0, keepdims=True)

        def search16(ref, base):
            def bit_body(t, tau):
                cand = tau + jnp.left_shift(jnp.int32(1), 15 - t)
                cand16 = cand.astype(I16)
                c = base + count16(ref, lambda blk: blk >= cand16)
                return jnp.where(c >= kf, cand, tau)
            return lax.fori_loop(0, 16, bit_body, jnp.full((1, tb), I16_MIN, I32))

        tau_hi = search16(sch, 0.0)
        tau_hi16 = tau_hi.astype(I16)
        above = count16(sch, lambda blk: blk > tau_hi16)
        for j in range(n_blocks):
            scl[j] = jnp.where(sch[j] == tau_hi16, scl[j], jnp.full((tb, tb), I16_MIN, I16))
        tau_lo = search16(scl, above)
        return tau_hi * 65536 + (tau_lo - I16_MIN)

    tau = lax.switch(i, [functools.partial(threshold_of, n) for n in range(off + 1, length // tb + 1)])
    cnt_ge = count(lambda blk, ks: blk >= tau)
    has_thr = tau > INT_MIN
    tie = has_thr & (cnt_ge > kf)
    any_tie = jnp.max(jnp.where(tie, 1.0, 0.0)) > 0.0

    def write_mask(sel_fn):
        def body(j, _):
            mk[j] = jnp.where(sel_fn(sc[j], j * tb), 0.0, NEG)
            return 0
        lax.fori_loop(0, nkb, body, 0)

    @pl.when(jnp.logical_not(any_tie))
    def _plain():
        thr = jnp.where(has_thr, tau, INT_MIN + 1)
        write_mask(lambda blk, ks: blk >= thr)

    @pl.when(any_tie)
    def _ties():
        need = kf - count(lambda blk, ks: blk > tau)
        q_idx = jnp.zeros((1, tb), I32)
        for bit in reversed(range(max(1, (length - 1).bit_length()))):
            cand = q_idx + (1 << bit)
            c = count(lambda blk, ks: (blk == tau) & ((key_l + ks) < cand))
            q_idx = jnp.where(c < need, cand, q_idx)
        last_eq = jnp.where(has_thr, jnp.where(tie, q_idx, length), -1)
        write_mask(lambda blk, ks: (blk > tau) | ((blk == tau) & ((key_l + ks) <= last_eq)))

    m_ref[...] = jnp.full(m_ref.shape, NEG, F32)
    l_ref[...] = jnp.zeros(l_ref.shape, F32)
    acc_ref[...] = jnp.zeros(acc_ref.shape, F32)
    row_lo = lax.broadcasted_iota(I32, (LANES, 1), 0) < HALF
    n_far = jnp.maximum(qblk - nd + 1, 0)

    def logits(j, d):
        ks = pl.multiple_of(j * tb, tb)
        kblk = k2[pl.ds(ks, tb), :]
        mblk = mk[j]
        for hp in range(H_B // 2):
            for par, qm in enumerate(head_queries(qb_ref, hp)):
                h = 2 * hp + par
                s = _nt_dot(kblk, qm) + mblk
                if d is not None:
                    s = s + bt_ref[h, d]
                s_ref[h, j] = s
                top = jnp.max(s, axis=0, keepdims=True)
                if d is None:
                    top = top + far_ref[h]
                m_ref[h] = jnp.maximum(m_ref[h], top)

    def weigh(j, far):
        ps = []
        for h in range(H_B):
            shift = m_ref[h] - far_ref[h] if far else m_ref[h]
            p = jnp.exp(s_ref[h, j] - shift)
            l_ref[h] = l_ref[h] + jnp.sum(p, axis=0, keepdims=True)
            ps.append(p.astype(BF16))
        for hp in range(H_B // 2):
            acc_ref[hp] = acc_ref[hp] + (_dot(vt_lo[j], ps[2 * hp]) + _dot(vt_hi[j], ps[2 * hp + 1]))

    def far_logits(j, _):
        logits(j, None)
        return 0

    lax.fori_loop(0, n_far, far_logits, 0)
    for d in range(nd):
        @pl.when(qblk - d >= 0)
        def _near_logits(d=d):
            logits(qblk - d, d)

    def far_weigh(j, _):
        weigh(j, True)
        return 0

    lax.fori_loop(0, n_far, far_weigh, 0)
    for d in range(nd):
        @pl.when(qblk - d >= 0)
        def _near_weigh(d=d):
            weigh(qblk - d, False)

    for hp in range(H_B // 2):
        denom = jnp.where(row_lo, l_ref[2 * hp], l_ref[2 * hp + 1])
        o_ref[0, :, hp * LANES:(hp + 1) * LANES] = (acc_ref[hp] / denom).T


def _sparse_attention(q_b, q_i, w_i, k2_new, v2_new, i2_new, k_past, v_past, i_past, bias_tiles, bias_far,
                      n_select):
    b, t, _ = q_b.shape
    has_past = k_past is not None
    p = k_past.shape[1] if has_past else 0
    tb = ATT_BLOCK
    assert t % tb == 0 and p % tb == 0 and tb % CHUNK == 0 and bias_tiles.shape[2] == tb
    length = p + t
    nd = bias_tiles.shape[1]
    nb = length // tb
    qspec = pl.BlockSpec((1, tb, W_B), lambda bi, i: (bi, i, 0))
    new = pl.BlockSpec((1, t, LANES), lambda bi, i: (bi, 0, 0))
    past = pl.BlockSpec((1, p, HD_B), lambda bi, i: (bi, 0, 0))
    ins = [q_b, q_i, w_i, k2_new, v2_new, i2_new] + ([k_past, v_past, i_past] if has_past else [])
    specs = ([qspec, qspec, pl.BlockSpec((1, tb, LANES), lambda bi, i: (bi, i, 0)), new, new, new]
             + ([past, past, past] if has_past else []))
    ins += [bias_tiles, bias_far]
    specs += [pl.BlockSpec(bias_tiles.shape, lambda bi, i: (0, 0, 0, 0)), pl.BlockSpec(memory_space=pltpu.SMEM)]
    return pl.pallas_call(
        functools.partial(_dsa_kernel, tb=tb, off=p // tb, past_len=p, has_past=has_past,
                          n_select=n_select, nd=nd, length=length),
        grid=(b, t // tb),
        in_specs=specs,
        out_specs=qspec,
        out_shape=jax.ShapeDtypeStruct((b, t, W_B), F32),
        scratch_shapes=[pltpu.VMEM((length, LANES), BF16),
                        pltpu.VMEM((nb, LANES, tb), BF16), pltpu.VMEM((nb, LANES, tb), BF16),
                        pltpu.VMEM((length, LANES), BF16),
                        pltpu.VMEM((nb, tb, tb), I32),
                        pltpu.VMEM((nb, tb, tb), I16), pltpu.VMEM((nb, tb, tb), I16),
                        pltpu.VMEM((nb, tb, tb), F32),
                        pltpu.VMEM((H_B, nb, tb, tb), F32),
                        pltpu.VMEM((H_B, 1, tb), F32), pltpu.VMEM((H_B, 1, tb), F32),
                        pltpu.VMEM((H_B // 2, LANES, tb), F32)],
        compiler_params=_params(("parallel", "arbitrary")),
        name=f"dsa_t{t}",
    )(*ins)


def _sigmoid(x):
    return 1.0 / (1.0 + jnp.exp(-x))


def _merge_kernel(x_ref, ya_ref, yb_ref, wg_ref, bg_ref, wpa_ref, wpb_ref, wo_ref, g_ref, b_ref, o_ref):
    x = x_ref[...]
    xb = x.astype(BF16)

    def seg(c0, n):
        return _dot(xb, wg_ref[:, c0:c0 + n]) + bg_ref[:, c0:c0 + n]

    g_a = seg(C_GA, W_A)
    y_a = (ya_ref[...] * (g_a * _sigmoid(g_a))).astype(BF16)
    branch_a = _sigmoid(seg(C_RA, D_MODEL)) * _dot(y_a, wpa_ref[...])
    g_b = seg(C_GB, W_B)
    y_b = (yb_ref[...] * (g_b * _sigmoid(g_b))).astype(BF16)
    branch_b = _sigmoid(seg(C_RB, D_MODEL)) * _dot(y_b, wpb_ref[...])
    merged = (branch_a + branch_b).astype(BF16)
    o_ref[...] = _ln(ALPHA * x + _dot(merged, wo_ref[...]), g_ref[...], b_ref[...])


def _merge(x2d, y_a, y_b, w_gate, b_gate, w_pa, w_pb, w_out, ln_g, ln_b):
    n = x2d.shape[0]
    tm = min(TOKEN_TILE, n)
    row = lambda width: pl.BlockSpec((tm, width), lambda i: (i, 0))
    full = lambda a: pl.BlockSpec(a.shape, lambda i: (0, 0))
    consts = [w_gate, b_gate, w_pa, w_pb, w_out, ln_g, ln_b]
    return pl.pallas_call(
        _merge_kernel,
        grid=(n // tm,),
        in_specs=[row(D_MODEL), row(W_A), row(W_B)] + [full(a) for a in consts],
        out_specs=row(D_MODEL),
        out_shape=jax.ShapeDtypeStruct((n, D_MODEL), F32),
        compiler_params=_params(("parallel",)),
        name=f"merge_n{n}",
    )(x2d, y_a, y_b, *consts)


def _pack_weights(w_in, b_in):
    offs = np.concatenate([[0], np.cumsum(SPLIT_SIZES)])
    names = ("q_a", "k_a", "v_a", "g_a", "q_b", "k_b", "v_b", "g_b", "q_i", "k_i", "w_i", "r_a", "r_b")
    w = {nm: w_in[:, :, offs[k]:offs[k + 1]] for k, nm in enumerate(names)}
    b = {nm: b_in[:, offs[k]:offs[k + 1]] for k, nm in enumerate(names)}
    pad = LANES - H_IDX

    def build(parts, last):
        order = (parts["q_a"] * SB_SCALE, parts["k_a"], parts["v_a"], parts["q_b"] * ATT_SCALE, parts["q_i"],
                 parts["k_b"], parts["k_b"], parts["v_b"], parts["v_b"], parts["k_i"], parts["k_i"], last)
        return jnp.concatenate(order, axis=-1)

    w_att = build(w, jnp.pad(w["w_i"], ((0, 0), (0, 0), (0, pad)))).astype(BF16)
    b_att = build(b, jnp.pad(b["w_i"], ((0, 0), (0, pad))))[:, None, :]
    w_gate = jnp.concatenate([w["g_a"], w["g_b"], w["r_a"], w["r_b"]], axis=-1).astype(BF16)
    b_gate = jnp.concatenate([b["g_a"], b["g_b"], b["r_a"], b["r_b"]], axis=-1)[:, None, :]
    return w_att, b_att, w_gate, b_gate


def _trunk_layer(x2d, batch, seq, layer, new_rows, past, n_select, weights, bias_tiles, bias_far):
    w_att, b_att, w_gate, b_gate, w_pa, w_pb, w_out, ln_g, ln_b = weights
    (q_a, q_b, q_i, kb2, vb2, ki2, w_i), new_rows = _project(x2d, w_att, b_att, new_rows, layer)
    shape3 = lambda a: a.reshape(batch, seq, a.shape[-1])
    stack4 = lambda a: a.reshape(DEPTH, batch, seq, a.shape[-1])
    if past is None:
        pk_a = pv_a = pk_b = pv_b = pk_i = None
    else:
        pk_a, pv_a, pk_b, pv_b, pk_i = past
        pk_a = pk_a.reshape(batch, -1, W_A)
        pv_a = pv_a.reshape(batch, -1, W_A)
    y_a = _stick_breaking(shape3(q_a), stack4(new_rows[0]), stack4(new_rows[1]), layer, pk_a, pv_a)
    y_b = _sparse_attention(shape3(q_b), shape3(q_i), shape3(w_i), shape3(kb2), shape3(vb2), shape3(ki2),
                            pk_b, pv_b, pk_i, bias_tiles, bias_far, n_select)
    x_next = _merge(x2d, y_a.reshape(-1, W_A), y_b.reshape(-1, W_B),
                    w_gate, b_gate, w_pa, w_pb, w_out, ln_g, ln_b)
    return x_next, new_rows


def _new_row_buffers(n):
    return tuple(jnp.zeros((DEPTH, n, width), F32) for width in (W_A, W_A, HD_B, HD_B, D_IDX))


def kernel(x_prompt, x_sample, cache_sb_k, cache_sb_v, cache_dsa_k, cache_dsa_v, cache_idx_k,
           ln_in_g, ln_in_b, w_in, b_in, w_proj_a, w_proj_b, w_out, ln_g, ln_b, rel_bias):
    batch, seq, _ = x_prompt.shape
    dec_batch, dec_seq, _ = x_sample.shape
    past_len = cache_sb_k.shape[2]
    dec_pad = -(-dec_seq // ATT_BLOCK) * ATT_BLOCK
    n_sel_prompt = min(MAX_SELECT, seq // 4)
    n_sel_sample = min(MAX_SELECT, (past_len + dec_seq) // 4)
    w_att, b_att, w_gate, b_gate = _pack_weights(w_in, b_in)
    w_pa = w_proj_a.astype(BF16)
    w_pb = w_proj_b.astype(BF16)
    w_o = w_out.astype(BF16)
    bias_far = rel_bias[N_BUCKETS // 2 - 1]
    tiles = _bias_tiles(rel_bias, ATT_BLOCK)
    hp = _layer_norm(x_prompt.reshape(-1, D_MODEL), ln_in_g, ln_in_b)
    xs = jnp.pad(x_sample, ((0, 0), (0, dec_pad - dec_seq), (0, 0)))
    hs = _layer_norm(xs.reshape(-1, D_MODEL), ln_in_g, ln_in_b)
    rows_p = _new_row_buffers(batch * seq)
    rows_s = _new_row_buffers(dec_batch * dec_pad)
    for layer in range(DEPTH):
        weights = (w_att[layer], b_att[layer], w_gate[layer], b_gate[layer], w_pa[layer], w_pb[layer], w_o[layer],
                   ln_g[layer].reshape(1, D_MODEL), ln_b[layer].reshape(1, D_MODEL))
        hp, rows_p = _trunk_layer(hp, batch, seq, layer, rows_p, None, n_sel_prompt, weights, tiles, bias_far)
        past = (cache_sb_k[layer], cache_sb_v[layer], cache_dsa_k[layer], cache_dsa_v[layer], cache_idx_k[layer])
        hs, rows_s = _trunk_layer(hs, dec_batch, dec_pad, layer, rows_s, past, n_sel_sample, weights, tiles, bias_far)

    def shaped(rows, b, t_pad, t):
        k_a, v_a, k_b, v_b, k_i = (r.reshape(DEPTH, b, t_pad, r.shape[-1])[:, :, :t] for r in rows)
        heads = lambda a: a.reshape(DEPTH, b, t, H_A, HD_A)
        return heads(k_a), heads(v_a), k_b, v_b, k_i

    y_s = hs.reshape(dec_batch, dec_pad, D_MODEL)[:, :dec_seq]
    return ((hp.reshape(batch, seq, D_MODEL), y_s) + shaped(rows_p, batch, seq, seq)
            + shaped(rows_s, dec_batch, dec_pad, dec_seq))
```

```python
import functools
import math

import jax
import jax.numpy as jnp
import numpy as np
from jax import lax
from jax.experimental import pallas as pl
from jax.experimental.pallas import tpu as pltpu

F32 = jnp.float32
BF16 = jnp.bfloat16
I32 = jnp.int32
I16 = jnp.int16

D_MODEL = 1024
DEPTH = 4
CHUNK = 64
H_A = 8
HD_A = 64
W_A = H_A * HD_A
H_B = 8
HD_B = 64
W_B = H_B * HD_B
H_IDX = 8
D_IDX = 64
MAX_SELECT = 256
N_BUCKETS = 32
MAX_DISTANCE = 128
LN_EPS = 1e-5
ALPHA = (2 * DEPTH) ** 0.25
SB_SCALE = HD_A ** -0.5
ATT_SCALE = HD_B ** -0.5
SPLIT_SIZES = (W_A, W_A, W_A, W_A, W_B, HD_B, HD_B, W_B, H_IDX * D_IDX, D_IDX, H_IDX, D_MODEL, D_MODEL)

LANES = 128
SUBLANES = 8
PACKED_ROWS = 2 * SUBLANES
HALF = 64
NEG = -1e30
INT_MIN = -(2 ** 31)
I16_MIN = -(2 ** 15)
EXP_UNDERFLOW = -105.0
V7X_VMEM_LIMIT = 56 * 1024 * 1024
ATT_BLOCK = 256
TOKEN_TILE = 1024

C_QA, C_KA, C_VA, C_QB, C_QI = 0, 512, 1024, 1536, 2048
C_KB2, C_VB2, C_KI2, C_WI = 2560, 2688, 2816, 2944
N_ATT = 3072
C_GA, C_GB, C_RA, C_RB = 0, 512, 1024, 2048
N_GATE = 3072


def _params(sem):
    return pltpu.CompilerParams(dimension_semantics=sem, vmem_limit_bytes=V7X_VMEM_LIMIT)


def _nt_dot(a, b):
    return lax.dot_general(a, b, (((1,), (1,)), ((), ())), preferred_element_type=F32)


def _dot(a, b):
    return jnp.dot(a, b, preferred_element_type=F32)


def _lane_lo():
    return lax.broadcasted_iota(I32, (1, LANES), 1) < HALF


def _ln(x, g, b):
    mu = jnp.mean(x, axis=-1, keepdims=True)
    xc = x - mu
    var = jnp.mean(xc * xc, axis=-1, keepdims=True)
    return xc * lax.rsqrt(var + LN_EPS) * g + b


def _ln_kernel(x_ref, g_ref, b_ref, o_ref):
    o_ref[...] = _ln(x_ref[...], g_ref[...], b_ref[...])


def _layer_norm(x2d, g, b):
    n = x2d.shape[0]
    tm = min(TOKEN_TILE, n)
    return pl.pallas_call(
        _ln_kernel,
        grid=(n // tm,),
        in_specs=[pl.BlockSpec((tm, D_MODEL), lambda i: (i, 0)),
                  pl.BlockSpec((1, D_MODEL), lambda i: (0, 0)),
                  pl.BlockSpec((1, D_MODEL), lambda i: (0, 0))],
        out_specs=pl.BlockSpec((tm, D_MODEL), lambda i: (i, 0)),
        out_shape=jax.ShapeDtypeStruct((n, D_MODEL), F32),
        compiler_params=_params(("parallel",)),
        name=f"ln_n{n}",
    )(x2d, g.reshape(1, D_MODEL), b.reshape(1, D_MODEL))


NEW_ROW_WIDTHS = (W_A, W_A, HD_B, HD_B, D_IDX)


def _proj_kernel(x_ref, w_ref, b_ref, ka_all, va_all, kb_all, vb_all, ki_all,
                 qa_ref, qb_ref, qi_ref, kb2_ref, vb2_ref, ki2_ref, wi_ref,
                 ka_ref, va_ref, kb_ref, vb_ref, ki_ref):
    del ka_all, va_all, kb_all, vb_all, ki_all
    xb = x_ref[...].astype(BF16)

    def seg(c0, n):
        return _dot(xb, w_ref[:, c0:c0 + n]) + b_ref[:, c0:c0 + n]

    qa_ref[...] = seg(C_QA, W_A).astype(BF16)
    ka_ref[...] = seg(C_KA, W_A)
    va_ref[...] = seg(C_VA, W_A)
    qb_ref[...] = seg(C_QB, W_B).astype(BF16)
    qi_ref[...] = seg(C_QI, H_IDX * D_IDX).astype(BF16)
    kb = seg(C_KB2, LANES)
    kb2_ref[...] = kb.astype(BF16)
    kb_ref[...] = kb[:, :HD_B]
    vb = seg(C_VB2, LANES)
    vb2_ref[...] = vb.astype(BF16)
    vb_ref[...] = vb[:, :HD_B]
    ki = seg(C_KI2, LANES)
    ki2_ref[...] = ki.astype(BF16)
    ki_ref[...] = ki[:, :D_IDX]
    wi_ref[...] = seg(C_WI, LANES)


def _project(x2d, w_att, b_att, new_rows, layer):
    n = x2d.shape[0]
    tm = min(TOKEN_TILE, n)
    row = lambda width: pl.BlockSpec((tm, width), lambda i: (i, 0))
    full = lambda a: pl.BlockSpec(a.shape, lambda i: (0, 0))
    sds = lambda width, dt: jax.ShapeDtypeStruct((n, width), dt)
    slab = lambda width: pl.BlockSpec((None, tm, width), lambda i: (layer, i, 0))
    n_plain = 7
    outs = pl.pallas_call(
        _proj_kernel,
        grid=(n // tm,),
        in_specs=[row(D_MODEL), full(w_att), full(b_att)] + [pl.BlockSpec(memory_space=pl.ANY)] * len(new_rows),
        out_specs=[row(W_A), row(W_B), row(H_IDX * D_IDX), row(LANES), row(LANES), row(LANES), row(LANES)]
                  + [slab(width) for width in NEW_ROW_WIDTHS],
        out_shape=[sds(W_A, BF16), sds(W_B, BF16), sds(H_IDX * D_IDX, BF16),
                   sds(LANES, BF16), sds(LANES, BF16), sds(LANES, BF16), sds(LANES, F32)]
                  + [jax.ShapeDtypeStruct(a.shape, a.dtype) for a in new_rows],
        input_output_aliases={3 + k: n_plain + k for k in range(len(new_rows))},
        compiler_params=_params(("parallel",)),
        name=f"proj_n{n}",
    )(x2d, w_att, b_att, *new_rows)
    return outs[:n_plain], tuple(outs[n_plain:])


def _sb_kernel(*refs, tb, off, past_len, has_past):
    if has_past:
        q_ref, kn_ref, vn_ref, kp_ref, vp_ref, o_ref, kbf, vt, tri, acc_ref = refs
    else:
        q_ref, kn_ref, vn_ref, o_ref, kbf, vt, tri, acc_ref = refs
    i = pl.program_id(1)
    lo = _lane_lo()
    n_pairs = W_A // LANES

    @pl.when(i == 0)
    def _fill():
        r = lax.broadcasted_iota(I32, (LANES, LANES), 0)
        c = lax.broadcasted_iota(I32, (LANES, LANES), 1)
        eye = jnp.where(r == c, 1.0, 0.0).astype(BF16)

        def put(j0, k, v):
            n = k.shape[0]
            kbf[j0 * tb:j0 * tb + n, :] = k.astype(BF16)
            for jb in range(n // tb):
                for hp in range(n_pairs):
                    blk = v[jb * tb:(jb + 1) * tb, hp * LANES:(hp + 1) * LANES].astype(BF16)
                    vt[j0 + jb, hp * LANES:(hp + 1) * LANES, :] = _nt_dot(eye, blk).astype(BF16)

        if has_past:
            put(0, kp_ref[0], vp_ref[0])
        put(past_len // tb, kn_ref[0], vn_ref[0])
        rr = lax.broadcasted_iota(I32, (tb, tb), 0)
        cc = lax.broadcasted_iota(I32, (tb, tb), 1)
        tri[...] = jnp.where(cc > rr, 1.0, 0.0).astype(BF16)

    key_l = lax.broadcasted_iota(I32, (tb, tb), 0)
    query_l = lax.broadcasted_iota(I32, (tb, tb), 1)
    before = key_l < query_l
    row_lo = lax.broadcasted_iota(I32, (LANES, 1), 0) < HALF
    qblk = i + off

    def block(j, carries, diag):
        ks = pl.multiple_of(j * tb, tb)
        heads = range(H_A)
        zs = []
        for h in heads:
            hp, par = divmod(h, 2)
            q2 = q_ref[0, :, hp * LANES:(hp + 1) * LANES]
            zero_q = jnp.zeros_like(q2)
            qh = jnp.where(lo, q2, zero_q) if par == 0 else jnp.where(lo, zero_q, q2)
            zs.append(_nt_dot(kbf[pl.ds(ks, tb), hp * LANES:(hp + 1) * LANES], qh))
        sp_first, his, los, logsig = [], [], [], []
        for h in heads:
            z = zs[h]
            sp = jnp.maximum(z, 0.0) + jnp.log(1.0 + jnp.exp(-jnp.abs(z)))
            logsig.append(z - sp)
            if diag:
                sp = jnp.where(before, sp, 0.0)
            hi = sp.astype(BF16)
            his.append(hi)
            los.append((sp - hi.astype(F32)).astype(BF16))
            sp_first.append(sp[0:1, :])
        sufs = [_dot(tri[...], his[h]) + _dot(tri[...], los[h]) for h in heads]
        weights = []
        for h in heads:
            a = jnp.exp(logsig[h] + (carries[h] - sufs[h]))
            if diag:
                a = jnp.where(before, a, 0.0)
            weights.append(a.astype(BF16))
        for hp in range(n_pairs):
            vtb = vt[j, hp * LANES:(hp + 1) * LANES, :]
            acc_ref[hp] = acc_ref[hp] + jnp.where(row_lo, _dot(vtb, weights[2 * hp]), _dot(vtb, weights[2 * hp + 1]))
        return tuple(carries[h] - (sufs[h][0:1, :] + sp_first[h]) for h in heads)

    def any_weight_left(carries):
        top = carries[0]
        for c in carries[1:]:
            top = jnp.maximum(top, c)
        return (jnp.max(top) > EXP_UNDERFLOW).astype(I32)

    acc_ref[...] = jnp.zeros(acc_ref.shape, F32)
    carries = block(qblk, tuple(jnp.zeros((1, tb), F32) for _ in range(H_A)), True)

    def cond(state):
        return jnp.logical_and(state[0] < qblk, state[1] > 0)

    def body(state):
        t = state[0]
        carries = block(qblk - 1 - t, state[2:], False)
        return (t + 1, any_weight_left(carries)) + carries

    lax.while_loop(cond, body, (jnp.int32(0), any_weight_left(carries)) + carries)
    for hp in range(n_pairs):
        o_ref[0, :, hp * LANES:(hp + 1) * LANES] = acc_ref[hp].T


def _stick_breaking(q, k_new, v_new, layer, k_past, v_past):
    b, t, _ = q.shape
    has_past = k_past is not None
    p = k_past.shape[1] if has_past else 0
    tb = ATT_BLOCK
    assert t % tb == 0 and p % tb == 0
    kv_new = pl.BlockSpec((None, 1, t, W_A), lambda bi, i: (layer, bi, 0, 0))
    kv_past = pl.BlockSpec((1, p, W_A), lambda bi, i: (bi, 0, 0))
    qo = pl.BlockSpec((1, tb, W_A), lambda bi, i: (bi, i, 0))
    ins = [q, k_new, v_new] + ([k_past, v_past] if has_past else [])
    specs = [qo, kv_new, kv_new] + ([kv_past, kv_past] if has_past else [])
    length = p + t
    return pl.pallas_call(
        functools.partial(_sb_kernel, tb=tb, off=p // tb, past_len=p, has_past=has_past),
        grid=(b, t // tb),
        in_specs=specs,
        out_specs=qo,
        out_shape=jax.ShapeDtypeStruct((b, t, W_A), F32),
        scratch_shapes=[pltpu.VMEM((length, W_A), BF16), pltpu.VMEM((length // tb, W_A, tb), BF16),
                        pltpu.VMEM((tb, tb), BF16), pltpu.VMEM((W_A // LANES, LANES, tb), F32)],
        compiler_params=_params(("parallel", "arbitrary")),
        name=f"sb_t{t}",
    )(*ins)


def _t5_bucket(rel):
    half = N_BUCKETS // 2
    max_exact = half // 2
    n = jnp.abs(rel)
    n_f = jnp.maximum(n, 1).astype(F32)
    large = max_exact + (jnp.log(n_f / max_exact) / math.log(MAX_DISTANCE / max_exact)
                         * (half - max_exact)).astype(I32)
    large = jnp.minimum(large, half - 1)
    return jnp.where(rel > 0, half, 0) + jnp.where(n < max_exact, n, large)


def _bias_kernel(rb_ref, o_ref, *, tb):
    h = pl.program_id(0)
    d = pl.program_id(1)
    key = lax.broadcasted_iota(I32, (tb, tb), 0)
    query = lax.broadcasted_iota(I32, (tb, tb), 1)
    bucket = _t5_bucket(key - query - d * tb)
    acc = jnp.zeros((tb, tb), F32)
    for bkt in range(N_BUCKETS):
        acc = jnp.where(bucket == bkt, rb_ref[bkt, h], acc)
    o_ref[0, 0] = acc


def _num_bias_diagonals(tb):
    return -(-MAX_DISTANCE // tb) + 1


def _bias_tiles(rel_bias, tb):
    nd = _num_bias_diagonals(tb)
    return pl.pallas_call(
        functools.partial(_bias_kernel, tb=tb),
        grid=(H_B, nd),
        in_specs=[pl.BlockSpec(memory_space=pltpu.SMEM)],
        out_specs=pl.BlockSpec((1, 1, tb, tb), lambda h, d: (h, d, 0, 0)),
        out_shape=jax.ShapeDtypeStruct((H_B, nd, tb, tb), F32),
        compiler_params=_params(("parallel", "parallel")),
        name=f"bias_tb{tb}",
    )(rel_bias)


def _dsa_kernel(*refs, tb, off, past_len, has_past, n_select, nd, length):
    if has_past:
        (qb_ref, qi_ref, wi_ref, kn_ref, vn_ref, in_ref, kp_ref, vp_ref, ip_ref, bt_ref, far_ref,
         o_ref, k2, vt_lo, vt_hi, i2, sc, sch, scl, mk, s_ref, m_ref, l_ref, acc_ref) = refs
    else:
        (qb_ref, qi_ref, wi_ref, kn_ref, vn_ref, in_ref, bt_ref, far_ref,
         o_ref, k2, vt_lo, vt_hi, i2, sc, sch, scl, mk, s_ref, m_ref, l_ref, acc_ref) = refs
    i = pl.program_id(1)
    lo = _lane_lo()
    qblk = i + off
    nkb = qblk + 1

    @pl.when(i == 0)
    def _fill():
        r = lax.broadcasted_iota(I32, (LANES, LANES), 0)
        c = lax.broadcasted_iota(I32, (LANES, LANES), 1)
        eye_lo = jnp.where((r == c) & (r < HALF), 1.0, 0.0).astype(BF16)
        eye_hi = jnp.where((r == c) & (r >= HALF), 1.0, 0.0).astype(BF16)

        def put(j0, kd, vd, idd):
            n = kd.shape[0]
            k2[j0 * tb:j0 * tb + n, :] = kd
            i2[j0 * tb:j0 * tb + n, :] = idd
            for jb in range(n // tb):
                blk = vd[jb * tb:(jb + 1) * tb, :]
                vt_lo[j0 + jb] = _nt_dot(eye_lo, blk).astype(BF16)
                vt_hi[j0 + jb] = _nt_dot(eye_hi, blk).astype(BF16)

        if has_past:
            rr = lax.broadcasted_iota(I32, (HALF, LANES), 0)
            cc = lax.broadcasted_iota(I32, (HALF, LANES), 1)
            dup = jnp.where((cc == rr) | (cc == rr + HALF), 1.0, 0.0).astype(BF16)
            widen = lambda x: _dot(x.astype(BF16), dup).astype(BF16)
            put(0, widen(kp_ref[0]), widen(vp_ref[0]), widen(ip_ref[0]))
        put(past_len // tb, kn_ref[0], vn_ref[0], in_ref[0])

    key_l = lax.broadcasted_iota(I32, (tb, tb), 0)
    query_l = lax.broadcasted_iota(I32, (tb, tb), 1)
    admissible = (key_l // CHUNK) <= (query_l // CHUNK)

    def head_queries(ref, hp):
        q2 = ref[0, :, hp * LANES:(hp + 1) * LANES]
        zero = jnp.zeros_like(q2)
        return jnp.where(lo, q2, zero), jnp.where(lo, zero, q2)

    wi_t = wi_ref[0].T

    def score_rows(j0, n_blk, diag_last):
        ks = pl.multiple_of(j0 * tb, tb)
        kib = i2[pl.ds(ks, n_blk * tb), :]
        dots = []
        for hp in range(H_IDX // 2):
            for qm in head_queries(qi_ref, hp):
                dots.append(_nt_dot(kib, qm))
        s = jnp.zeros((n_blk * tb, tb), F32)
        for h in range(H_IDX):
            s = s + wi_t[h:h + 1, :] * jnp.maximum(dots[h], 0.0)
        bits = pltpu.bitcast(s, I32)
        keys = bits ^ ((bits >> 31) & 0x7FFFFFFF)
        for b in range(n_blk):
            key = keys[b * tb:(b + 1) * tb, :]
            if diag_last and b == n_blk - 1:
                key = jnp.where(admissible, key, INT_MIN)
            sc[j0 + b] = key
            sch[j0 + b] = (key >> 16).astype(I16)
            scl[j0 + b] = ((key & 0xFFFF) + I16_MIN).astype(I16)

    def score_pair(t, _):
        score_rows(2 * t, 2, False)
        return 0

    lax.fori_loop(0, qblk // 2, score_pair, 0)

    @pl.when(qblk % 2 == 1)
    def _score_tail_pair():
        score_rows(qblk - 1, 2, True)

    @pl.when(qblk % 2 == 0)
    def _score_tail_single():
        score_rows(qblk, 1, True)

    kf = float(n_select)

    def count(pred_fn):
        def body(j, c):
            hit = pred_fn(sc[j], j * tb)
            ones = jnp.where(hit, 1.0, 0.0)
            return c + jnp.sum(ones.reshape(tb // SUBLANES, SUBLANES, tb), axis=0)
        c8 = lax.fori_loop(0, nkb, body, jnp.zeros((SUBLANES, tb), F32))
        return jnp.sum(c8, axis=0, keepdims=True)

    def threshold_of(n_blocks):
        def count16(ref, pred_fn):
            c = jnp.zeros((PACKED_ROWS, tb), I16)
            for j in range(n_blocks):
                ones = jnp.where(pred_fn(ref[j]), jnp.ones((tb, tb), I16), jnp.zeros((tb, tb), I16))
                for g in range(tb // PACKED_ROWS):
                    c = c + ones[g * PACKED_ROWS:(g + 1) * PACKED_ROWS, :]
            return jnp.sum(c.astype(F32), axis=0, keepdims=True)

        def search16(ref, base):
            def bit_body(t, tau):
                cand = tau + jnp.left_shift(jnp.int32(1), 15 - t)
                cand16 = cand.astype(I16)
                c = base + count16(ref, lambda blk: blk >= cand16)
                return jnp.where(c >= kf, cand, tau)
            return lax.fori_loop(0, 16, bit_body, jnp.full((1, tb), I16_MIN, I32))

        tau_hi = search16(sch, 0.0)
        tau_hi16 = tau_hi.astype(I16)
        above = count16(sch, lambda blk: blk > tau_hi16)
        for j in range(n_blocks):
            scl[j] = jnp.where(sch[j] == tau_hi16, scl[j], jnp.full((tb, tb), I16_MIN, I16))
        tau_lo = search16(scl, above)
        return tau_hi * 65536 + (tau_lo - I16_MIN)

    tau = lax.switch(i, [functools.partial(threshold_of, n) for n in range(off + 1, length // tb + 1)])
    cnt_ge = count(lambda blk, ks: blk >= tau)
    has_thr = tau > INT_MIN
    tie = has_thr & (cnt_ge > kf)
    any_tie = jnp.max(jnp.where(tie, 1.0, 0.0)) > 0.0

    def write_mask(sel_fn):
        def body(j, _):
            mk[j] = jnp.where(sel_fn(sc[j], j * tb), 0.0, NEG)
            return 0
        lax.fori_loop(0, nkb, body, 0)

    @pl.when(jnp.logical_not(any_tie))
    def _plain():
        thr = jnp.where(has_thr, tau, INT_MIN + 1)
        write_mask(lambda blk, ks: blk >= thr)

    @pl.when(any_tie)
    def _ties():
        need = kf - count(lambda blk, ks: blk > tau)
        q_idx = jnp.zeros((1, tb), I32)
        for bit in reversed(range(max(1, (length - 1).bit_length()))):
            cand = q_idx + (1 << bit)
            c = count(lambda blk, ks: (blk == tau) & ((key_l + ks) < cand))
            q_idx = jnp.where(c < need, cand, q_idx)
        last_eq = jnp.where(has_thr, jnp.where(tie, q_idx, length), -1)
        write_mask(lambda blk, ks: (blk > tau) | ((blk == tau) & ((key_l + ks) <= last_eq)))

    m_ref[...] = jnp.full(m_ref.shape, NEG, F32)
    l_ref[...] = jnp.zeros(l_ref.shape, F32)
    acc_ref[...] = jnp.zeros(acc_ref.shape, F32)
    row_lo = lax.broadcasted_iota(I32, (LANES, 1), 0) < HALF
    n_far = jnp.maximum(qblk - nd + 1, 0)

    def logits_rows(j0, ds):
        n_blk = len(ds)
        ks = pl.multiple_of(j0 * tb, tb)
        kblk = k2[pl.ds(ks, n_blk * tb), :]
        for hp in range(H_B // 2):
            for par, qm in enumerate(head_queries(qb_ref, hp)):
                h = 2 * hp + par
                rows = _nt_dot(kblk, qm)
                for b, d in enumerate(ds):
                    s = rows[b * tb:(b + 1) * tb, :] + mk[j0 + b]
                    if d is not None:
                        s = s + bt_ref[h, d]
                    s_ref[h, j0 + b] = s
                    top = jnp.max(s, axis=0, keepdims=True)
                    if d is None:
                        top = top + far_ref[h]
                    m_ref[h] = jnp.maximum(m_ref[h], top)

    def weigh(j, far):
        ps = []
        for h in range(H_B):
            shift = m_ref[h] - far_ref[h] if far else m_ref[h]
            p = jnp.exp(s_ref[h, j] - shift)
            l_ref[h] = l_ref[h] + jnp.sum(p, axis=0, keepdims=True)
            ps.append(p.astype(BF16))
        for hp in range(H_B // 2):
            acc_ref[hp] = acc_ref[hp] + (_dot(vt_lo[j], ps[2 * hp]) + _dot(vt_hi[j], ps[2 * hp + 1]))

    assert nd == 2

    def far_logits_pair(t, _):
        logits_rows(2 * t, (None, None))
        return 0

    lax.fori_loop(0, n_far // 2, far_logits_pair, 0)

    @pl.when(n_far % 2 == 1)
    def _far_logits_single():
        logits_rows(n_far - 1, (None,))

    @pl.when(qblk >= 1)
    def _near_logits_pair():
        logits_rows(qblk - 1, (1, 0))

    @pl.when(qblk == 0)
    def _near_logits_single():
        logits_rows(qblk, (0,))

    def far_weigh(j, _):
        weigh(j, True)
        return 0

    lax.fori_loop(0, n_far, far_weigh, 0)
    for d in range(nd):
        @pl.when(qblk - d >= 0)
        def _near_weigh(d=d):
            weigh(qblk - d, False)

    for hp in range(H_B // 2):
        denom = jnp.where(row_lo, l_ref[2 * hp], l_ref[2 * hp + 1])
        o_ref[0, :, hp * LANES:(hp + 1) * LANES] = (acc_ref[hp] / denom).T


def _sparse_attention(q_b, q_i, w_i, k2_new, v2_new, i2_new, k_past, v_past, i_past, bias_tiles, bias_far,
                      n_select):
    b, t, _ = q_b.shape
    has_past = k_past is not None
    p = k_past.shape[1] if has_past else 0
    tb = ATT_BLOCK
    assert t % tb == 0 and p % tb == 0 and tb % CHUNK == 0 and bias_tiles.shape[2] == tb
    length = p + t
    nd = bias_tiles.shape[1]
    nb = length // tb
    qspec = pl.BlockSpec((1, tb, W_B), lambda bi, i: (bi, i, 0))
    new = pl.BlockSpec((1, t, LANES), lambda bi, i: (bi, 0, 0))
    past = pl.BlockSpec((1, p, HD_B), lambda bi, i: (bi, 0, 0))
    ins = [q_b, q_i, w_i, k2_new, v2_new, i2_new] + ([k_past, v_past, i_past] if has_past else [])
    specs = ([qspec, qspec, pl.BlockSpec((1, tb, LANES), lambda bi, i: (bi, i, 0)), new, new, new]
             + ([past, past, past] if has_past else []))
    ins += [bias_tiles, bias_far]
    specs += [pl.BlockSpec(bias_tiles.shape, lambda bi, i: (0, 0, 0, 0)), pl.BlockSpec(memory_space=pltpu.SMEM)]
    return pl.pallas_call(
        functools.partial(_dsa_kernel, tb=tb, off=p // tb, past_len=p, has_past=has_past,
                          n_select=n_select, nd=nd, length=length),
        grid=(b, t // tb),
        in_specs=specs,
        out_specs=qspec,
        out_shape=jax.ShapeDtypeStruct((b, t, W_B), F32),
        scratch_shapes=[pltpu.VMEM((length, LANES), BF16),
                        pltpu.VMEM((nb, LANES, tb), BF16), pltpu.VMEM((nb, LANES, tb), BF16),
                        pltpu.VMEM((length, LANES), BF16),
                        pltpu.VMEM((nb, tb, tb), I32),
                        pltpu.VMEM((nb, tb, tb), I16), pltpu.VMEM((nb, tb, tb), I16),
                        pltpu.VMEM((nb, tb, tb), F32),
                        pltpu.VMEM((H_B, nb, tb, tb), F32),
                        pltpu.VMEM((H_B, 1, tb), F32), pltpu.VMEM((H_B, 1, tb), F32),
                        pltpu.VMEM((H_B // 2, LANES, tb), F32)],
        compiler_params=_params(("parallel", "arbitrary")),
        name=f"dsa_t{t}",
    )(*ins)


def _sigmoid(x):
    return 1.0 / (1.0 + jnp.exp(-x))


def _merge_kernel(x_ref, ya_ref, yb_ref, wg_ref, bg_ref, wpa_ref, wpb_ref, wo_ref, g_ref, b_ref, o_ref):
    x = x_ref[...]
    xb = x.astype(BF16)

    def seg(c0, n):
        return _dot(xb, wg_ref[:, c0:c0 + n]) + bg_ref[:, c0:c0 + n]

    g_a = seg(C_GA, W_A)
    y_a = (ya_ref[...] * (g_a * _sigmoid(g_a))).astype(BF16)
    branch_a = _sigmoid(seg(C_RA, D_MODEL)) * _dot(y_a, wpa_ref[...])
    g_b = seg(C_GB, W_B)
    y_b = (yb_ref[...] * (g_b * _sigmoid(g_b))).astype(BF16)
    branch_b = _sigmoid(seg(C_RB, D_MODEL)) * _dot(y_b, wpb_ref[...])
    merged = (branch_a + branch_b).astype(BF16)
    o_ref[...] = _ln(ALPHA * x + _dot(merged, wo_ref[...]), g_ref[...], b_ref[...])


def _merge(x2d, y_a, y_b, w_gate, b_gate, w_pa, w_pb, w_out, ln_g, ln_b):
    n = x2d.shape[0]
    tm = min(TOKEN_TILE, n)
    row = lambda width: pl.BlockSpec((tm, width), lambda i: (i, 0))
    full = lambda a: pl.BlockSpec(a.shape, lambda i: (0, 0))
    consts = [w_gate, b_gate, w_pa, w_pb, w_out, ln_g, ln_b]
    return pl.pallas_call(
        _merge_kernel,
        grid=(n // tm,),
        in_specs=[row(D_MODEL), row(W_A), row(W_B)] + [full(a) for a in consts],
        out_specs=row(D_MODEL),
        out_shape=jax.ShapeDtypeStruct((n, D_MODEL), F32),
        compiler_params=_params(("parallel",)),
        name=f"merge_n{n}",
    )(x2d, y_a, y_b, *consts)


def _pack_weights(w_in, b_in):
    offs = np.concatenate([[0], np.cumsum(SPLIT_SIZES)])
    names = ("q_a", "k_a", "v_a", "g_a", "q_b", "k_b", "v_b", "g_b", "q_i", "k_i", "w_i", "r_a", "r_b")
    w = {nm: w_in[:, :, offs[k]:offs[k + 1]] for k, nm in enumerate(names)}
    b = {nm: b_in[:, offs[k]:offs[k + 1]] for k, nm in enumerate(names)}
    pad = LANES - H_IDX

    def build(parts, last):
        order = (parts["q_a"] * SB_SCALE, parts["k_a"], parts["v_a"], parts["q_b"] * ATT_SCALE, parts["q_i"],
                 parts["k_b"], parts["k_b"], parts["v_b"], parts["v_b"], parts["k_i"], parts["k_i"], last)
        return jnp.concatenate(order, axis=-1)

    w_att = build(w, jnp.pad(w["w_i"], ((0, 0), (0, 0), (0, pad)))).astype(BF16)
    b_att = build(b, jnp.pad(b["w_i"], ((0, 0), (0, pad))))[:, None, :]
    w_gate = jnp.concatenate([w["g_a"], w["g_b"], w["r_a"], w["r_b"]], axis=-1).astype(BF16)
    b_gate = jnp.concatenate([b["g_a"], b["g_b"], b["r_a"], b["r_b"]], axis=-1)[:, None, :]
    return w_att, b_att, w_gate, b_gate


def _trunk_layer(x2d, batch, seq, layer, new_rows, past, n_select, weights, bias_tiles, bias_far):
    w_att, b_att, w_gate, b_gate, w_pa, w_pb, w_out, ln_g, ln_b = weights
    (q_a, q_b, q_i, kb2, vb2, ki2, w_i), new_rows = _project(x2d, w_att, b_att, new_rows, layer)
    shape3 = lambda a: a.reshape(batch, seq, a.shape[-1])
    stack4 = lambda a: a.reshape(DEPTH, batch, seq, a.shape[-1])
    if past is None:
        pk_a = pv_a = pk_b = pv_b = pk_i = None
    else:
        pk_a, pv_a, pk_b, pv_b, pk_i = past
        pk_a = pk_a.reshape(batch, -1, W_A)
        pv_a = pv_a.reshape(batch, -1, W_A)
    y_a = _stick_breaking(shape3(q_a), stack4(new_rows[0]), stack4(new_rows[1]), layer, pk_a, pv_a)
    y_b = _sparse_attention(shape3(q_b), shape3(q_i), shape3(w_i), shape3(kb2), shape3(vb2), shape3(ki2),
                            pk_b, pv_b, pk_i, bias_tiles, bias_far, n_select)
    x_next = _merge(x2d, y_a.reshape(-1, W_A), y_b.reshape(-1, W_B),
                    w_gate, b_gate, w_pa, w_pb, w_out, ln_g, ln_b)
    return x_next, new_rows


def kernel(x_prompt, x_sample, cache_sb_k, cache_sb_v, cache_dsa_k, cache_dsa_v, cache_idx_k,
           ln_in_g, ln_in_b, w_in, b_in, w_proj_a, w_proj_b, w_out, ln_g, ln_b, rel_bias):
    batch, seq, _ = x_prompt.shape
    dec_batch, dec_seq, _ = x_sample.shape
    past_len = cache_sb_k.shape[2]
    dec_pad = -(-dec_seq // ATT_BLOCK) * ATT_BLOCK
    n_sel_prompt = min(MAX_SELECT, seq // 4)
    n_sel_sample = min(MAX_SELECT, (past_len + dec_seq) // 4)
    w_att, b_att, w_gate, b_gate = _pack_weights(w_in, b_in)
    w_pa = w_proj_a.astype(BF16)
    w_pb = w_proj_b.astype(BF16)
    w_o = w_out.astype(BF16)
    bias_far = rel_bias[N_BUCKETS // 2 - 1]
    tiles = _bias_tiles(rel_bias, ATT_BLOCK)
    hp = _layer_norm(x_prompt.reshape(-1, D_MODEL), ln_in_g, ln_in_b)
    xs = jnp.pad(x_sample, ((0, 0), (0, dec_pad - dec_seq), (0, 0)))
    hs = _layer_norm(xs.reshape(-1, D_MODEL), ln_in_g, ln_in_b)
    rows_p = tuple(jnp.zeros((DEPTH, batch * seq, width), F32) for width in NEW_ROW_WIDTHS)
    rows_s = tuple(jnp.zeros((DEPTH, dec_batch * dec_pad, width), F32) for width in NEW_ROW_WIDTHS)
    for layer in range(DEPTH):
        weights = (w_att[layer], b_att[layer], w_gate[layer], b_gate[layer], w_pa[layer], w_pb[layer], w_o[layer],
                   ln_g[layer].reshape(1, D_MODEL), ln_b[layer].reshape(1, D_MODEL))
        hp, rows_p = _trunk_layer(hp, batch, seq, layer, rows_p, None, n_sel_prompt, weights, tiles, bias_far)
        past = (cache_sb_k[layer], cache_sb_v[layer], cache_dsa_k[layer], cache_dsa_v[layer], cache_idx_k[layer])
        hs, rows_s = _trunk_layer(hs, dec_batch, dec_pad, layer, rows_s, past, n_sel_sample, weights, tiles, bias_far)

    def shaped(rows, b, t_pad, t):
        k_a, v_a, k_b, v_b, k_i = (r.reshape(DEPTH, b, t_pad, r.shape[-1])[:, :, :t] for r in rows)
        heads = lambda a: a.reshape(DEPTH, b, t, H_A, HD_A)
        return heads(k_a), heads(v_a), k_b, v_b, k_i

    y_s = hs.reshape(dec_batch, dec_pad, D_MODEL)[:, :dec_seq]
    return ((hp.reshape(batch, seq, D_MODEL), y_s) + shaped(rows_p, batch, seq, seq)
            + shaped(rows_s, dec_batch, dec_pad, dec_seq))
```

```python
import functools
import math

import jax
import jax.numpy as jnp
import numpy as np
from jax import lax
from jax.experimental import pallas as pl
from jax.experimental.pallas import tpu as pltpu

F32 = jnp.float32
BF16 = jnp.bfloat16
I32 = jnp.int32
I16 = jnp.int16

D_MODEL = 1024
DEPTH = 4
CHUNK = 64
H_A = 8
HD_A = 64
W_A = H_A * HD_A
H_B = 8
HD_B = 64
W_B = H_B * HD_B
H_IDX = 8
D_IDX = 64
MAX_SELECT = 256
N_BUCKETS = 32
MAX_DISTANCE = 128
LN_EPS = 1e-5
ALPHA = (2 * DEPTH) ** 0.25
SB_SCALE = HD_A ** -0.5
ATT_SCALE = HD_B ** -0.5
SPLIT_SIZES = (W_A, W_A, W_A, W_A, W_B, HD_B, HD_B, W_B, H_IDX * D_IDX, D_IDX, H_IDX, D_MODEL, D_MODEL)

LANES = 128
SUBLANES = 8
PACKED_ROWS = 2 * SUBLANES
HALF = 64
NEG = -1e30
INT_MIN = -(2 ** 31)
I16_MIN = -(2 ** 15)
EXP_UNDERFLOW = -105.0
V7X_VMEM_LIMIT = 56 * 1024 * 1024
ATT_BLOCK = 256
TOKEN_TILE = 1024

C_QA, C_KA, C_VA, C_QB, C_QI = 0, 512, 1024, 1536, 2048
C_KB2, C_VB2, C_KI2, C_WI = 2560, 2688, 2816, 2944
N_ATT = 3072
C_GA, C_GB, C_RA, C_RB = 0, 512, 1024, 2048
N_GATE = 3072


def _params(sem):
    return pltpu.CompilerParams(dimension_semantics=sem, vmem_limit_bytes=V7X_VMEM_LIMIT)


def _nt_dot(a, b):
    return lax.dot_general(a, b, (((1,), (1,)), ((), ())), preferred_element_type=F32)


def _dot(a, b):
    return jnp.dot(a, b, preferred_element_type=F32)


def _lane_lo():
    return lax.broadcasted_iota(I32, (1, LANES), 1) < HALF


def _ln(x, g, b):
    mu = jnp.mean(x, axis=-1, keepdims=True)
    xc = x - mu
    var = jnp.mean(xc * xc, axis=-1, keepdims=True)
    return xc * lax.rsqrt(var + LN_EPS) * g + b


def _ln_kernel(x_ref, g_ref, b_ref, o_ref):
    o_ref[...] = _ln(x_ref[...], g_ref[...], b_ref[...])


def _layer_norm(x2d, g, b):
    n = x2d.shape[0]
    tm = min(TOKEN_TILE, n)
    return pl.pallas_call(
        _ln_kernel,
        grid=(n // tm,),
        in_specs=[pl.BlockSpec((tm, D_MODEL), lambda i: (i, 0)),
                  pl.BlockSpec((1, D_MODEL), lambda i: (0, 0)),
                  pl.BlockSpec((1, D_MODEL), lambda i: (0, 0))],
        out_specs=pl.BlockSpec((tm, D_MODEL), lambda i: (i, 0)),
        out_shape=jax.ShapeDtypeStruct((n, D_MODEL), F32),
        compiler_params=_params(("parallel",)),
        name=f"ln_n{n}",
    )(x2d, g.reshape(1, D_MODEL), b.reshape(1, D_MODEL))


NEW_ROW_WIDTHS = (W_A, W_A, HD_B, HD_B, D_IDX)


def _proj_kernel(x_ref, w_ref, b_ref, qa_ref, qb_ref, qi_ref, kb2_ref, vb2_ref, ki2_ref, wi_ref,
                 ka_ref, va_ref, kb_ref, vb_ref, ki_ref):
    xb = x_ref[...].astype(BF16)

    def seg(c0, n):
        return _dot(xb, w_ref[:, c0:c0 + n]) + b_ref[:, c0:c0 + n]

    qa_ref[...] = seg(C_QA, W_A).astype(BF16)
    ka_ref[...] = seg(C_KA, W_A)
    va_ref[...] = seg(C_VA, W_A)
    qb_ref[...] = seg(C_QB, W_B).astype(BF16)
    qi_ref[...] = seg(C_QI, H_IDX * D_IDX).astype(BF16)
    kb = seg(C_KB2, LANES)
    kb2_ref[...] = kb.astype(BF16)
    kb_ref[...] = kb[:, :HD_B]
    vb = seg(C_VB2, LANES)
    vb2_ref[...] = vb.astype(BF16)
    vb_ref[...] = vb[:, :HD_B]
    ki = seg(C_KI2, LANES)
    ki2_ref[...] = ki.astype(BF16)
    ki_ref[...] = ki[:, :D_IDX]
    wi_ref[...] = seg(C_WI, LANES)


def _project(x2d, w_att, b_att):
    n = x2d.shape[0]
    tm = min(TOKEN_TILE, n)
    row = lambda width: pl.BlockSpec((tm, width), lambda i: (i, 0))
    full = lambda a: pl.BlockSpec(a.shape, lambda i: (0, 0))
    sds = lambda width, dt: jax.ShapeDtypeStruct((n, width), dt)
    n_plain = 7
    outs = pl.pallas_call(
        _proj_kernel,
        grid=(n // tm,),
        in_specs=[row(D_MODEL), full(w_att), full(b_att)],
        out_specs=[row(W_A), row(W_B), row(H_IDX * D_IDX), row(LANES), row(LANES), row(LANES), row(LANES)]
                  + [row(width) for width in NEW_ROW_WIDTHS],
        out_shape=[sds(W_A, BF16), sds(W_B, BF16), sds(H_IDX * D_IDX, BF16),
                   sds(LANES, BF16), sds(LANES, BF16), sds(LANES, BF16), sds(LANES, F32)]
                  + [sds(width, F32) for width in NEW_ROW_WIDTHS],
        compiler_params=_params(("parallel",)),
        name=f"proj_n{n}",
    )(x2d, w_att, b_att)
    return outs[:n_plain], tuple(outs[n_plain:])


def _collect_kernel(*refs):
    ins, (ka_o, va_o, kb_o, vb_o, ki_o) = refs[:-5], refs[-5:]
    for layer in range(DEPTH):
        ka, va, kb, vb, ki = ins[5 * layer:5 * layer + 5]
        tm = ka.shape[0]
        for src, dst in ((ka, ka_o), (va, va_o)):
            rows = src[...]
            for h in range(H_A):
                dst[layer, pl.ds(h, tm, stride=H_A), :] = rows[:, h * HD_A:(h + 1) * HD_A]
        kb_o[layer] = kb[...]
        vb_o[layer] = vb[...]
        ki_o[layer] = ki[...]


COLLECT_TILE = 256


def _collect(rows_per_layer):
    n = rows_per_layer[0][0].shape[0]
    tm = min(COLLECT_TILE, n)
    flat = [a for rows in rows_per_layer for a in rows]
    in_specs = [pl.BlockSpec((tm, a.shape[1]), lambda i: (i, 0)) for a in flat]
    heads = pl.BlockSpec((DEPTH, tm * H_A, HD_A), lambda i: (0, i, 0))
    narrow = lambda width: pl.BlockSpec((DEPTH, tm, width), lambda i: (0, i, 0))
    return pl.pallas_call(
        _collect_kernel,
        grid=(n // tm,),
        in_specs=in_specs,
        out_specs=[heads, heads, narrow(HD_B), narrow(HD_B), narrow(D_IDX)],
        out_shape=[jax.ShapeDtypeStruct((DEPTH, n * H_A, HD_A), F32), jax.ShapeDtypeStruct((DEPTH, n * H_A, HD_A), F32),
                   jax.ShapeDtypeStruct((DEPTH, n, HD_B), F32), jax.ShapeDtypeStruct((DEPTH, n, HD_B), F32),
                   jax.ShapeDtypeStruct((DEPTH, n, D_IDX), F32)],
        compiler_params=_params(("parallel",)),
        name=f"collect_n{n}",
    )(*flat)


def _sb_kernel(*refs, tb, off, past_len, has_past):
    if has_past:
        q_ref, kn_ref, vn_ref, kp_ref, vp_ref, o_ref, kbf, vt, tri, acc_ref = refs
    else:
        q_ref, kn_ref, vn_ref, o_ref, kbf, vt, tri, acc_ref = refs
    i = pl.program_id(1)
    lo = _lane_lo()
    n_pairs = W_A // LANES

    @pl.when(i == 0)
    def _fill():
        r = lax.broadcasted_iota(I32, (LANES, LANES), 0)
        c = lax.broadcasted_iota(I32, (LANES, LANES), 1)
        eye = jnp.where(r == c, 1.0, 0.0).astype(BF16)

        def put(j0, k, v):
            n = k.shape[0]
            kbf[j0 * tb:j0 * tb + n, :] = k.astype(BF16)
            for jb in range(n // tb):
                for hp in range(n_pairs):
                    blk = v[jb * tb:(jb + 1) * tb, hp * LANES:(hp + 1) * LANES].astype(BF16)
                    vt[j0 + jb, hp * LANES:(hp + 1) * LANES, :] = _nt_dot(eye, blk).astype(BF16)

        if has_past:
            put(0, kp_ref[0], vp_ref[0])
        put(past_len // tb, kn_ref[0], vn_ref[0])
        rr = lax.broadcasted_iota(I32, (tb, tb), 0)
        cc = lax.broadcasted_iota(I32, (tb, tb), 1)
        tri[...] = jnp.where(cc > rr, 1.0, 0.0).astype(BF16)

    key_l = lax.broadcasted_iota(I32, (tb, tb), 0)
    query_l = lax.broadcasted_iota(I32, (tb, tb), 1)
    before = key_l < query_l
    row_lo = lax.broadcasted_iota(I32, (LANES, 1), 0) < HALF
    qblk = i + off

    def block(j, carries, diag):
        ks = pl.multiple_of(j * tb, tb)
        heads = range(H_A)
        zs = []
        for h in heads:
            hp, par = divmod(h, 2)
            q2 = q_ref[0, :, hp * LANES:(hp + 1) * LANES]
            zero_q = jnp.zeros_like(q2)
            qh = jnp.where(lo, q2, zero_q) if par == 0 else jnp.where(lo, zero_q, q2)
            zs.append(_nt_dot(kbf[pl.ds(ks, tb), hp * LANES:(hp + 1) * LANES], qh))
        sp_first, his, los, logsig = [], [], [], []
        for h in heads:
            z = zs[h]
            sp = jnp.maximum(z, 0.0) + jnp.log(1.0 + jnp.exp(-jnp.abs(z)))
            logsig.append(z - sp)
            if diag:
                sp = jnp.where(before, sp, 0.0)
            hi = sp.astype(BF16)
            his.append(hi)
            los.append((sp - hi.astype(F32)).astype(BF16))
            sp_first.append(sp[0:1, :])
        sufs = [_dot(tri[...], his[h]) + _dot(tri[...], los[h]) for h in heads]
        weights = []
        for h in heads:
            a = jnp.exp(logsig[h] + (carries[h] - sufs[h]))
            if diag:
                a = jnp.where(before, a, 0.0)
            weights.append(a.astype(BF16))
        for hp in range(n_pairs):
            vtb = vt[j, hp * LANES:(hp + 1) * LANES, :]
            acc_ref[hp] = acc_ref[hp] + jnp.where(row_lo, _dot(vtb, weights[2 * hp]), _dot(vtb, weights[2 * hp + 1]))
        return tuple(carries[h] - (sufs[h][0:1, :] + sp_first[h]) for h in heads)

    def any_weight_left(carries):
        top = carries[0]
        for c in carries[1:]:
            top = jnp.maximum(top, c)
        return (jnp.max(top) > EXP_UNDERFLOW).astype(I32)

    acc_ref[...] = jnp.zeros(acc_ref.shape, F32)
    carries = block(qblk, tuple(jnp.zeros((1, tb), F32) for _ in range(H_A)), True)

    def cond(state):
        return jnp.logical_and(state[0] < qblk, state[1] > 0)

    def body(state):
        t = state[0]
        carries = block(qblk - 1 - t, state[2:], False)
        return (t + 1, any_weight_left(carries)) + carries

    lax.while_loop(cond, body, (jnp.int32(0), any_weight_left(carries)) + carries)
    for hp in range(n_pairs):
        o_ref[0, :, hp * LANES:(hp + 1) * LANES] = acc_ref[hp].T


def _stick_breaking(q, k_new, v_new, k_past, v_past):
    b, t, _ = q.shape
    has_past = k_past is not None
    p = k_past.shape[1] if has_past else 0
    tb = ATT_BLOCK
    assert t % tb == 0 and p % tb == 0
    kv_new = pl.BlockSpec((1, t, W_A), lambda bi, i: (bi, 0, 0))
    kv_past = pl.BlockSpec((1, p, W_A), lambda bi, i: (bi, 0, 0))
    qo = pl.BlockSpec((1, tb, W_A), lambda bi, i: (bi, i, 0))
    ins = [q, k_new, v_new] + ([k_past, v_past] if has_past else [])
    specs = [qo, kv_new, kv_new] + ([kv_past, kv_past] if has_past else [])
    length = p + t
    return pl.pallas_call(
        functools.partial(_sb_kernel, tb=tb, off=p // tb, past_len=p, has_past=has_past),
        grid=(b, t // tb),
        in_specs=specs,
        out_specs=qo,
        out_shape=jax.ShapeDtypeStruct((b, t, W_A), F32),
        scratch_shapes=[pltpu.VMEM((length, W_A), BF16), pltpu.VMEM((length // tb, W_A, tb), BF16),
                        pltpu.VMEM((tb, tb), BF16), pltpu.VMEM((W_A // LANES, LANES, tb), F32)],
        compiler_params=_params(("parallel", "arbitrary")),
        name=f"sb_t{t}",
    )(*ins)


def _t5_bucket(rel):
    half = N_BUCKETS // 2
    max_exact = half // 2
    n = jnp.abs(rel)
    n_f = jnp.maximum(n, 1).astype(F32)
    large = max_exact + (jnp.log(n_f / max_exact) / math.log(MAX_DISTANCE / max_exact)
                         * (half - max_exact)).astype(I32)
    large = jnp.minimum(large, half - 1)
    return jnp.where(rel > 0, half, 0) + jnp.where(n < max_exact, n, large)


def _bias_kernel(rb_ref, o_ref, *, tb):
    h = pl.program_id(0)
    d = pl.program_id(1)
    key = lax.broadcasted_iota(I32, (tb, tb), 0)
    query = lax.broadcasted_iota(I32, (tb, tb), 1)
    bucket = _t5_bucket(key - query - d * tb)
    acc = jnp.zeros((tb, tb), F32)
    for bkt in range(N_BUCKETS):
        acc = jnp.where(bucket == bkt, rb_ref[bkt, h], acc)
    o_ref[0, 0] = acc


def _num_bias_diagonals(tb):
    return -(-MAX_DISTANCE // tb) + 1


def _bias_tiles(rel_bias, tb):
    nd = _num_bias_diagonals(tb)
    return pl.pallas_call(
        functools.partial(_bias_kernel, tb=tb),
        grid=(H_B, nd),
        in_specs=[pl.BlockSpec(memory_space=pltpu.SMEM)],
        out_specs=pl.BlockSpec((1, 1, tb, tb), lambda h, d: (h, d, 0, 0)),
        out_shape=jax.ShapeDtypeStruct((H_B, nd, tb, tb), F32),
        compiler_params=_params(("parallel", "parallel")),
        name=f"bias_tb{tb}",
    )(rel_bias)


def _dsa_kernel(*refs, tb, off, past_len, has_past, n_select, nd, length):
    if has_past:
        (qb_ref, qi_ref, wi_ref, kn_ref, vn_ref, in_ref, kp_ref, vp_ref, ip_ref, bt_ref, far_ref,
         o_ref, k2, vt_lo, vt_hi, i2, sc, sch, scl, mk, s_ref, m_ref, l_ref, acc_ref) = refs
    else:
        (qb_ref, qi_ref, wi_ref, kn_ref, vn_ref, in_ref, bt_ref, far_ref,
         o_ref, k2, vt_lo, vt_hi, i2, sc, sch, scl, mk, s_ref, m_ref, l_ref, acc_ref) = refs
    i = pl.program_id(1)
    lo = _lane_lo()
    qblk = i + off
    nkb = qblk + 1

    @pl.when(i == 0)
    def _fill():
        r = lax.broadcasted_iota(I32, (LANES, LANES), 0)
        c = lax.broadcasted_iota(I32, (LANES, LANES), 1)
        eye_lo = jnp.where((r == c) & (r < HALF), 1.0, 0.0).astype(BF16)
        eye_hi = jnp.where((r == c) & (r >= HALF), 1.0, 0.0).astype(BF16)

        def put(j0, kd, vd, idd):
            n = kd.shape[0]
            k2[j0 * tb:j0 * tb + n, :] = kd
            i2[j0 * tb:j0 * tb + n, :] = idd
            for jb in range(n // tb):
                blk = vd[jb * tb:(jb + 1) * tb, :]
                vt_lo[j0 + jb] = _nt_dot(eye_lo, blk).astype(BF16)
                vt_hi[j0 + jb] = _nt_dot(eye_hi, blk).astype(BF16)

        if has_past:
            rr = lax.broadcasted_iota(I32, (HALF, LANES), 0)
            cc = lax.broadcasted_iota(I32, (HALF, LANES), 1)
            dup = jnp.where((cc == rr) | (cc == rr + HALF), 1.0, 0.0).astype(BF16)
            widen = lambda x: _dot(x.astype(BF16), dup).astype(BF16)
            put(0, widen(kp_ref[0]), widen(vp_ref[0]), widen(ip_ref[0]))
        put(past_len // tb, kn_ref[0], vn_ref[0], in_ref[0])

    key_l = lax.broadcasted_iota(I32, (tb, tb), 0)
    query_l = lax.broadcasted_iota(I32, (tb, tb), 1)
    admissible = (key_l // CHUNK) <= (query_l // CHUNK)

    def head_queries(ref, hp):
        q2 = ref[0, :, hp * LANES:(hp + 1) * LANES]
        zero = jnp.zeros_like(q2)
        return jnp.where(lo, q2, zero), jnp.where(lo, zero, q2)

    wi_t = wi_ref[0].T

    def score_rows(j0, n_blk, diag_last):
        ks = pl.multiple_of(j0 * tb, tb)
        kib = i2[pl.ds(ks, n_blk * tb), :]
        dots = []
        for hp in range(H_IDX // 2):
            for qm in head_queries(qi_ref, hp):
                dots.append(_nt_dot(kib, qm))
        s = jnp.zeros((n_blk * tb, tb), F32)
        for h in range(H_IDX):
            s = s + wi_t[h:h + 1, :] * jnp.maximum(dots[h], 0.0)
        bits = pltpu.bitcast(s, I32)
        keys = bits ^ ((bits >> 31) & 0x7FFFFFFF)
        for b in range(n_blk):
            key = keys[b * tb:(b + 1) * tb, :]
            if diag_last and b == n_blk - 1:
                key = jnp.where(admissible, key, INT_MIN)
            sc[j0 + b] = key
            sch[j0 + b] = (key >> 16).astype(I16)
            scl[j0 + b] = ((key & 0xFFFF) + I16_MIN).astype(I16)

    def score_pair(t, _):
        score_rows(2 * t, 2, False)
        return 0

    lax.fori_loop(0, qblk // 2, score_pair, 0)

    @pl.when(qblk % 2 == 1)
    def _score_tail_pair():
        score_rows(qblk - 1, 2, True)

    @pl.when(qblk % 2 == 0)
    def _score_tail_single():
        score_rows(qblk, 1, True)

    kf = float(n_select)

    def count(pred_fn):
        def body(j, c):
            hit = pred_fn(sc[j], j * tb)
            ones = jnp.where(hit, 1.0, 0.0)
            return c + jnp.sum(ones.reshape(tb // SUBLANES, SUBLANES, tb), axis=0)
        c8 = lax.fori_loop(0, nkb, body, jnp.zeros((SUBLANES, tb), F32))
        return jnp.sum(c8, axis=0, keepdims=True)

    def threshold_of(n_blocks):
        def count16(ref, pred_fn):
            c = jnp.zeros((PACKED_ROWS, tb), I16)
            for j in range(n_blocks):
                ones = jnp.where(pred_fn(ref[j]), jnp.ones((tb, tb), I16), jnp.zeros((tb, tb), I16))
                for g in range(tb // PACKED_ROWS):
                    c = c + ones[g * PACKED_ROWS:(g + 1) * PACKED_ROWS, :]
            return jnp.sum(c.astype(F32), axis=0, keepdims=True)

        def search16(ref, base):
            def bit_body(t, tau):
                cand = tau + jnp.left_shift(jnp.int32(1), 15 - t)
                cand16 = cand.astype(I16)
                c = base + count16(ref, lambda blk: blk >= cand16)
                return jnp.where(c >= kf, cand, tau)
            return lax.fori_loop(0, 16, bit_body, jnp.full((1, tb), I16_MIN, I32))

        tau_hi = search16(sch, 0.0)
        tau_hi16 = tau_hi.astype(I16)
        above = count16(sch, lambda blk: blk > tau_hi16)
        for j in range(n_blocks):
            scl[j] = jnp.where(sch[j] == tau_hi16, scl[j], jnp.full((tb, tb), I16_MIN, I16))
        tau_lo = search16(scl, above)
        return tau_hi * 65536 + (tau_lo - I16_MIN)

    tau = lax.switch(i, [functools.partial(threshold_of, n) for n in range(off + 1, length // tb + 1)])
    cnt_ge = count(lambda blk, ks: blk >= tau)
    has_thr = tau > INT_MIN
    tie = has_thr & (cnt_ge > kf)
    any_tie = jnp.max(jnp.where(tie, 1.0, 0.0)) > 0.0

    def write_mask(sel_fn):
        def body(j, _):
            mk[j] = jnp.where(sel_fn(sc[j], j * tb), 0.0, NEG)
            return 0
        lax.fori_loop(0, nkb, body, 0)

    @pl.when(jnp.logical_not(any_tie))
    def _plain():
        thr = jnp.where(has_thr, tau, INT_MIN + 1)
        write_mask(lambda blk, ks: blk >= thr)

    @pl.when(any_tie)
    def _ties():
        need = kf - count(lambda blk, ks: blk > tau)
        q_idx = jnp.zeros((1, tb), I32)
        for bit in reversed(range(max(1, (length - 1).bit_length()))):
            cand = q_idx + (1 << bit)
            c = count(lambda blk, ks: (blk == tau) & ((key_l + ks) < cand))
            q_idx = jnp.where(c < need, cand, q_idx)
        last_eq = jnp.where(has_thr, jnp.where(tie, q_idx, length), -1)
        write_mask(lambda blk, ks: (blk > tau) | ((blk == tau) & ((key_l + ks) <= last_eq)))

    m_ref[...] = jnp.full(m_ref.shape, NEG, F32)
    l_ref[...] = jnp.zeros(l_ref.shape, F32)
    acc_ref[...] = jnp.zeros(acc_ref.shape, F32)
    row_lo = lax.broadcasted_iota(I32, (LANES, 1), 0) < HALF
    n_far = jnp.maximum(qblk - nd + 1, 0)

    def logits_rows(j0, ds):
        n_blk = len(ds)
        ks = pl.multiple_of(j0 * tb, tb)
        kblk = k2[pl.ds(ks, n_blk * tb), :]
        for hp in range(H_B // 2):
            for par, qm in enumerate(head_queries(qb_ref, hp)):
                h = 2 * hp + par
                rows = _nt_dot(kblk, qm)
                for b, d in enumerate(ds):
                    s = rows[b * tb:(b + 1) * tb, :] + mk[j0 + b]
                    if d is not None:
                        s = s + bt_ref[h, d]
                    s_ref[h, j0 + b] = s
                    top = jnp.max(s, axis=0, keepdims=True)
                    if d is None:
                        top = top + far_ref[h]
                    m_ref[h] = jnp.maximum(m_ref[h], top)

    def weigh_rows(j0, fars):
        ps = [[] for _ in fars]
        for h in range(H_B):
            for b, far in enumerate(fars):
                shift = m_ref[h] - far_ref[h] if far else m_ref[h]
                p = jnp.exp(s_ref[h, j0 + b] - shift)
                l_ref[h] = l_ref[h] + jnp.sum(p, axis=0, keepdims=True)
                ps[b].append(p.astype(BF16))
        for hp in range(H_B // 2):
            pv = None
            for b in range(len(fars)):
                part = _dot(vt_lo[j0 + b], ps[b][2 * hp]) + _dot(vt_hi[j0 + b], ps[b][2 * hp + 1])
                pv = part if pv is None else pv + part
            acc_ref[hp] = acc_ref[hp] + pv

    assert nd == 2

    def far_logits_pair(t, _):
        logits_rows(2 * t, (None, None))
        return 0

    lax.fori_loop(0, n_far // 2, far_logits_pair, 0)

    @pl.when(n_far % 2 == 1)
    def _far_logits_single():
        logits_rows(n_far - 1, (None,))

    @pl.when(qblk >= 1)
    def _near_logits_pair():
        logits_rows(qblk - 1, (1, 0))

    @pl.when(qblk == 0)
    def _near_logits_single():
        logits_rows(qblk, (0,))

    def far_weigh_pair(t, _):
        weigh_rows(2 * t, (True, True))
        return 0

    lax.fori_loop(0, n_far // 2, far_weigh_pair, 0)

    @pl.when(n_far % 2 == 1)
    def _far_weigh_single():
        weigh_rows(n_far - 1, (True,))

    @pl.when(qblk >= 1)
    def _near_weigh_pair():
        weigh_rows(qblk - 1, (False, False))

    @pl.when(qblk == 0)
    def _near_weigh_single():
        weigh_rows(qblk, (False,))

    for hp in range(H_B // 2):
        denom = jnp.where(row_lo, l_ref[2 * hp], l_ref[2 * hp + 1])
        o_ref[0, :, hp * LANES:(hp + 1) * LANES] = (acc_ref[hp] / denom).T


def _sparse_attention(q_b, q_i, w_i, k2_new, v2_new, i2_new, k_past, v_past, i_past, bias_tiles, bias_far,
                      n_select):
    b, t, _ = q_b.shape
    has_past = k_past is not None
    p = k_past.shape[1] if has_past else 0
    tb = ATT_BLOCK
    assert t % tb == 0 and p % tb == 0 and tb % CHUNK == 0 and bias_tiles.shape[2] == tb
    length = p + t
    nd = bias_tiles.shape[1]
    nb = length // tb
    qspec = pl.BlockSpec((1, tb, W_B), lambda bi, i: (bi, i, 0))
    new = pl.BlockSpec((1, t, LANES), lambda bi, i: (bi, 0, 0))
    past = pl.BlockSpec((1, p, HD_B), lambda bi, i: (bi, 0, 0))
    ins = [q_b, q_i, w_i, k2_new, v2_new, i2_new] + ([k_past, v_past, i_past] if has_past else [])
    specs = ([qspec, qspec, pl.BlockSpec((1, tb, LANES), lambda bi, i: (bi, i, 0)), new, new, new]
             + ([past, past, past] if has_past else []))
    ins += [bias_tiles, bias_far]
    specs += [pl.BlockSpec(bias_tiles.shape, lambda bi, i: (0, 0, 0, 0)), pl.BlockSpec(memory_space=pltpu.SMEM)]
    return pl.pallas_call(
        functools.partial(_dsa_kernel, tb=tb, off=p // tb, past_len=p, has_past=has_past,
                          n_select=n_select, nd=nd, length=length),
        grid=(b, t // tb),
        in_specs=specs,
        out_specs=qspec,
        out_shape=jax.ShapeDtypeStruct((b, t, W_B), F32),
        scratch_shapes=[pltpu.VMEM((length, LANES), BF16),
                        pltpu.VMEM((nb, LANES, tb), BF16), pltpu.VMEM((nb, LANES, tb), BF16),
                        pltpu.VMEM((length, LANES), BF16),
                        pltpu.VMEM((nb, tb, tb), I32),
                        pltpu.VMEM((nb, tb, tb), I16), pltpu.VMEM((nb, tb, tb), I16),
                        pltpu.VMEM((nb, tb, tb), F32),
                        pltpu.VMEM((H_B, nb, tb, tb), F32),
                        pltpu.VMEM((H_B, 1, tb), F32), pltpu.VMEM((H_B, 1, tb), F32),
                        pltpu.VMEM((H_B // 2, LANES, tb), F32)],
        compiler_params=_params(("parallel", "arbitrary")),
        name=f"dsa_t{t}",
    )(*ins)


def _sigmoid(x):
    return 1.0 / (1.0 + jnp.exp(-x))


def _merge_kernel(x_ref, ya_ref, yb_ref, wg_ref, bg_ref, wpa_ref, wpb_ref, wo_ref, g_ref, b_ref, o_ref):
    x = x_ref[...]
    xb = x.astype(BF16)

    def seg(c0, n):
        return _dot(xb, wg_ref[:, c0:c0 + n]) + bg_ref[:, c0:c0 + n]

    g_a = seg(C_GA, W_A)
    y_a = (ya_ref[...] * (g_a * _sigmoid(g_a))).astype(BF16)
    branch_a = _sigmoid(seg(C_RA, D_MODEL)) * _dot(y_a, wpa_ref[...])
    g_b = seg(C_GB, W_B)
    y_b = (yb_ref[...] * (g_b * _sigmoid(g_b))).astype(BF16)
    branch_b = _sigmoid(seg(C_RB, D_MODEL)) * _dot(y_b, wpb_ref[...])
    merged = (branch_a + branch_b).astype(BF16)
    o_ref[...] = _ln(ALPHA * x + _dot(merged, wo_ref[...]), g_ref[...], b_ref[...])


def _merge(x2d, y_a, y_b, w_gate, b_gate, w_pa, w_pb, w_out, ln_g, ln_b):
    n = x2d.shape[0]
    tm = min(TOKEN_TILE, n)
    row = lambda width: pl.BlockSpec((tm, width), lambda i: (i, 0))
    full = lambda a: pl.BlockSpec(a.shape, lambda i: (0, 0))
    consts = [w_gate, b_gate, w_pa, w_pb, w_out, ln_g, ln_b]
    return pl.pallas_call(
        _merge_kernel,
        grid=(n // tm,),
        in_specs=[row(D_MODEL), row(W_A), row(W_B)] + [full(a) for a in consts],
        out_specs=row(D_MODEL),
        out_shape=jax.ShapeDtypeStruct((n, D_MODEL), F32),
        compiler_params=_params(("parallel",)),
        name=f"merge_n{n}",
    )(x2d, y_a, y_b, *consts)


def _pack_weights(w_in, b_in):
    offs = np.concatenate([[0], np.cumsum(SPLIT_SIZES)])
    names = ("q_a", "k_a", "v_a", "g_a", "q_b", "k_b", "v_b", "g_b", "q_i", "k_i", "w_i", "r_a", "r_b")
    w = {nm: w_in[:, :, offs[k]:offs[k + 1]] for k, nm in enumerate(names)}
    b = {nm: b_in[:, offs[k]:offs[k + 1]] for k, nm in enumerate(names)}
    pad = LANES - H_IDX

    def build(parts, last):
        order = (parts["q_a"] * SB_SCALE, parts["k_a"], parts["v_a"], parts["q_b"] * ATT_SCALE, parts["q_i"],
                 parts["k_b"], parts["k_b"], parts["v_b"], parts["v_b"], parts["k_i"], parts["k_i"], last)
        return jnp.concatenate(order, axis=-1)

    w_att = build(w, jnp.pad(w["w_i"], ((0, 0), (0, 0), (0, pad)))).astype(BF16)
    b_att = build(b, jnp.pad(b["w_i"], ((0, 0), (0, pad))))[:, None, :]
    w_gate = jnp.concatenate([w["g_a"], w["g_b"], w["r_a"], w["r_b"]], axis=-1).astype(BF16)
    b_gate = jnp.concatenate([b["g_a"], b["g_b"], b["r_a"], b["r_b"]], axis=-1)[:, None, :]
    return w_att, b_att, w_gate, b_gate


def _trunk_layer(x2d, batch, seq, past, n_select, weights, bias_tiles, bias_far):
    w_att, b_att, w_gate, b_gate, w_pa, w_pb, w_out, ln_g, ln_b = weights
    (q_a, q_b, q_i, kb2, vb2, ki2, w_i), new_rows = _project(x2d, w_att, b_att)
    shape3 = lambda a: a.reshape(batch, seq, a.shape[-1])
    if past is None:
        pk_a = pv_a = pk_b = pv_b = pk_i = None
    else:
        pk_a, pv_a, pk_b, pv_b, pk_i = past
        pk_a = pk_a.reshape(batch, -1, W_A)
        pv_a = pv_a.reshape(batch, -1, W_A)
    y_a = _stick_breaking(shape3(q_a), shape3(new_rows[0]), shape3(new_rows[1]), pk_a, pv_a)
    y_b = _sparse_attention(shape3(q_b), shape3(q_i), shape3(w_i), shape3(kb2), shape3(vb2), shape3(ki2),
                            pk_b, pv_b, pk_i, bias_tiles, bias_far, n_select)
    x_next = _merge(x2d, y_a.reshape(-1, W_A), y_b.reshape(-1, W_B),
                    w_gate, b_gate, w_pa, w_pb, w_out, ln_g, ln_b)
    return x_next, new_rows


def kernel(x_prompt, x_sample, cache_sb_k, cache_sb_v, cache_dsa_k, cache_dsa_v, cache_idx_k,
           ln_in_g, ln_in_b, w_in, b_in, w_proj_a, w_proj_b, w_out, ln_g, ln_b, rel_bias):
    batch, seq, _ = x_prompt.shape
    dec_batch, dec_seq, _ = x_sample.shape
    past_len = cache_sb_k.shape[2]
    dec_pad = -(-dec_seq // ATT_BLOCK) * ATT_BLOCK
    n_sel_prompt = min(MAX_SELECT, seq // 4)
    n_sel_sample = min(MAX_SELECT, (past_len + dec_seq) // 4)
    w_att, b_att, w_gate, b_gate = _pack_weights(w_in, b_in)
    w_pa = w_proj_a.astype(BF16)
    w_pb = w_proj_b.astype(BF16)
    w_o = w_out.astype(BF16)
    bias_far = rel_bias[N_BUCKETS // 2 - 1]
    tiles = _bias_tiles(rel_bias, ATT_BLOCK)
    hp = _layer_norm(x_prompt.reshape(-1, D_MODEL), ln_in_g, ln_in_b)
    xs = jnp.pad(x_sample, ((0, 0), (0, dec_pad - dec_seq), (0, 0)))
    hs = _layer_norm(xs.reshape(-1, D_MODEL), ln_in_g, ln_in_b)
    rows_p, rows_s = [], []
    for layer in range(DEPTH):
        weights = (w_att[layer], b_att[layer], w_gate[layer], b_gate[layer], w_pa[layer], w_pb[layer], w_o[layer],
                   ln_g[layer].reshape(1, D_MODEL), ln_b[layer].reshape(1, D_MODEL))
        hp, new_p = _trunk_layer(hp, batch, seq, None, n_sel_prompt, weights, tiles, bias_far)
        past = (cache_sb_k[layer], cache_sb_v[layer], cache_dsa_k[layer], cache_dsa_v[layer], cache_idx_k[layer])
        hs, new_s = _trunk_layer(hs, dec_batch, dec_pad, past, n_sel_sample, weights, tiles, bias_far)
        rows_p.append(new_p)
        rows_s.append(new_s)

    def shaped(stacked, b, t_pad, t):
        k_a, v_a, k_b, v_b, k_i = stacked
        heads = lambda a: a.reshape(DEPTH, b, t_pad, H_A, HD_A)[:, :, :t]
        flat = lambda a: a.reshape(DEPTH, b, t_pad, a.shape[-1])[:, :, :t]
        return heads(k_a), heads(v_a), flat(k_b), flat(v_b), flat(k_i)

    y_s = hs.reshape(dec_batch, dec_pad, D_MODEL)[:, :dec_seq]
    return ((hp.reshape(batch, seq, D_MODEL), y_s) + shaped(_collect(rows_p), batch, seq, seq)
            + shaped(_collect(rows_s), dec_batch, dec_pad, dec_seq))
```

```python
import functools
import math

import jax
import jax.numpy as jnp
import numpy as np
from jax import lax
from jax.experimental import pallas as pl
from jax.experimental.pallas import tpu as pltpu

F32 = jnp.float32
BF16 = jnp.bfloat16
I32 = jnp.int32
I16 = jnp.int16

D_MODEL = 1024
DEPTH = 4
CHUNK = 64
H_A = 8
HD_A = 64
W_A = H_A * HD_A
H_B = 8
HD_B = 64
W_B = H_B * HD_B
H_IDX = 8
D_IDX = 64
MAX_SELECT = 256
N_BUCKETS = 32
MAX_DISTANCE = 128
LN_EPS = 1e-5
ALPHA = (2 * DEPTH) ** 0.25
SB_SCALE = HD_A ** -0.5
ATT_SCALE = HD_B ** -0.5
SPLIT_SIZES = (W_A, W_A, W_A, W_A, W_B, HD_B, HD_B, W_B, H_IDX * D_IDX, D_IDX, H_IDX, D_MODEL, D_MODEL)

LANES = 128
SUBLANES = 8
PACKED_ROWS = 2 * SUBLANES
HALF = 64
NEG = -1e30
INT_MIN = -(2 ** 31)
I16_MIN = -(2 ** 15)
EXP_UNDERFLOW = -105.0
V7X_VMEM_LIMIT = 56 * 1024 * 1024
ATT_BLOCK = 256
TOKEN_TILE = 1024

C_QA, C_KA, C_VA, C_QB, C_QI = 0, 512, 1024, 1536, 2048
C_KB2, C_VB2, C_KI2, C_WI = 2560, 2688, 2816, 2944
N_ATT = 3072
C_GA, C_GB, C_RA, C_RB = 0, 512, 1024, 2048
N_GATE = 3072


def _params(sem):
    return pltpu.CompilerParams(dimension_semantics=sem, vmem_limit_bytes=V7X_VMEM_LIMIT)


def _nt_dot(a, b):
    return lax.dot_general(a, b, (((1,), (1,)), ((), ())), preferred_element_type=F32)


def _dot(a, b):
    return jnp.dot(a, b, preferred_element_type=F32)


def _lane_lo():
    return lax.broadcasted_iota(I32, (1, LANES), 1) < HALF


def _ln(x, g, b):
    mu = jnp.mean(x, axis=-1, keepdims=True)
    xc = x - mu
    var = jnp.mean(xc * xc, axis=-1, keepdims=True)
    return xc * lax.rsqrt(var + LN_EPS) * g + b


NEW_ROW_WIDTHS = (W_A, W_A, HD_B, HD_B, D_IDX)


def _proj_kernel(*refs, norm_input):
    if norm_input:
        x_ref, g_ref, beta_ref, w_ref, b_ref, *outs, h_ref = refs
        x = _ln(x_ref[...], g_ref[...], beta_ref[...])
        h_ref[...] = x
    else:
        x_ref, w_ref, b_ref, *outs = refs
        x = x_ref[...]
    (qa_ref, qb_ref, qi_ref, kb2_ref, vb2_ref, ki2_ref, wi_ref, ka_ref, va_ref, kb_ref, vb_ref, ki_ref) = outs
    xb = x.astype(BF16)

    def seg(c0, n):
        return _dot(xb, w_ref[:, c0:c0 + n]) + b_ref[:, c0:c0 + n]

    qa_ref[...] = seg(C_QA, W_A).astype(BF16)
    ka_ref[...] = seg(C_KA, W_A)
    va_ref[...] = seg(C_VA, W_A)
    qb_ref[...] = seg(C_QB, W_B).astype(BF16)
    qi_ref[...] = seg(C_QI, H_IDX * D_IDX).astype(BF16)
    kb = seg(C_KB2, LANES)
    kb2_ref[...] = kb.astype(BF16)
    kb_ref[...] = kb[:, :HD_B]
    vb = seg(C_VB2, LANES)
    vb2_ref[...] = vb.astype(BF16)
    vb_ref[...] = vb[:, :HD_B]
    ki = seg(C_KI2, LANES)
    ki2_ref[...] = ki.astype(BF16)
    ki_ref[...] = ki[:, :D_IDX]
    wi_ref[...] = seg(C_WI, LANES)


def _project(x2d, w_att, b_att, ln_in=None):
    n = x2d.shape[0]
    tm = min(TOKEN_TILE, n)
    row = lambda width: pl.BlockSpec((tm, width), lambda i: (i, 0))
    full = lambda a: pl.BlockSpec(a.shape, lambda i: (0, 0))
    sds = lambda width, dt: jax.ShapeDtypeStruct((n, width), dt)
    n_plain = 7
    norm_input = ln_in is not None
    ln_args = [a.reshape(1, D_MODEL) for a in ln_in] if norm_input else []
    outs = pl.pallas_call(
        functools.partial(_proj_kernel, norm_input=norm_input),
        grid=(n // tm,),
        in_specs=[row(D_MODEL)] + [full(a) for a in ln_args] + [full(w_att), full(b_att)],
        out_specs=[row(W_A), row(W_B), row(H_IDX * D_IDX), row(LANES), row(LANES), row(LANES), row(LANES)]
                  + [row(width) for width in NEW_ROW_WIDTHS] + ([row(D_MODEL)] if norm_input else []),
        out_shape=[sds(W_A, BF16), sds(W_B, BF16), sds(H_IDX * D_IDX, BF16),
                   sds(LANES, BF16), sds(LANES, BF16), sds(LANES, BF16), sds(LANES, F32)]
                  + [sds(width, F32) for width in NEW_ROW_WIDTHS] + ([sds(D_MODEL, F32)] if norm_input else []),
        compiler_params=_params(("parallel",)),
        name=f"proj_n{n}",
    )(x2d, *ln_args, w_att, b_att)
    n_rows = len(NEW_ROW_WIDTHS)
    return outs[:n_plain], tuple(outs[n_plain:n_plain + n_rows]), (outs[-1] if norm_input else x2d)


def _collect_kernel(*refs):
    ins, (ka_o, va_o, kb_o, vb_o, ki_o) = refs[:-5], refs[-5:]
    for layer in range(DEPTH):
        ka, va, kb, vb, ki = ins[5 * layer:5 * layer + 5]
        tm = ka.shape[0]
        for src, dst in ((ka, ka_o), (va, va_o)):
            rows = src[...]
            for h in range(H_A):
                dst[layer, pl.ds(h, tm, stride=H_A), :] = rows[:, h * HD_A:(h + 1) * HD_A]
        kb_o[layer] = kb[...]
        vb_o[layer] = vb[...]
        ki_o[layer] = ki[...]


COLLECT_TILE = 256


def _collect(rows_per_layer):
    n = rows_per_layer[0][0].shape[0]
    tm = min(COLLECT_TILE, n)
    flat = [a for rows in rows_per_layer for a in rows]
    in_specs = [pl.BlockSpec((tm, a.shape[1]), lambda i: (i, 0)) for a in flat]
    heads = pl.BlockSpec((DEPTH, tm * H_A, HD_A), lambda i: (0, i, 0))
    narrow = lambda width: pl.BlockSpec((DEPTH, tm, width), lambda i: (0, i, 0))
    return pl.pallas_call(
        _collect_kernel,
        grid=(n // tm,),
        in_specs=in_specs,
        out_specs=[heads, heads, narrow(HD_B), narrow(HD_B), narrow(D_IDX)],
        out_shape=[jax.ShapeDtypeStruct((DEPTH, n * H_A, HD_A), F32), jax.ShapeDtypeStruct((DEPTH, n * H_A, HD_A), F32),
                   jax.ShapeDtypeStruct((DEPTH, n, HD_B), F32), jax.ShapeDtypeStruct((DEPTH, n, HD_B), F32),
                   jax.ShapeDtypeStruct((DEPTH, n, D_IDX), F32)],
        compiler_params=_params(("parallel",)),
        name=f"collect_n{n}",
    )(*flat)


def _sb_kernel(*refs, tb, off, past_len, has_past):
    if has_past:
        q_ref, kn_ref, vn_ref, kp_ref, vp_ref, o_ref, kbf, vt, tri, acc_ref = refs
    else:
        q_ref, kn_ref, vn_ref, o_ref, kbf, vt, tri, acc_ref = refs
    i = pl.program_id(1)
    lo = _lane_lo()
    n_pairs = W_A // LANES

    @pl.when(i == 0)
    def _fill():
        r = lax.broadcasted_iota(I32, (LANES, LANES), 0)
        c = lax.broadcasted_iota(I32, (LANES, LANES), 1)
        eye = jnp.where(r == c, 1.0, 0.0).astype(BF16)

        def put(j0, k, v):
            n = k.shape[0]
            kbf[j0 * tb:j0 * tb + n, :] = k.astype(BF16)
            for jb in range(n // tb):
                for hp in range(n_pairs):
                    blk = v[jb * tb:(jb + 1) * tb, hp * LANES:(hp + 1) * LANES].astype(BF16)
                    vt[j0 + jb, hp * LANES:(hp + 1) * LANES, :] = _nt_dot(eye, blk).astype(BF16)

        if has_past:
            put(0, kp_ref[0], vp_ref[0])
        put(past_len // tb, kn_ref[0], vn_ref[0])
        rr = lax.broadcasted_iota(I32, (tb, tb), 0)
        cc = lax.broadcasted_iota(I32, (tb, tb), 1)
        tri[...] = jnp.where(cc > rr, 1.0, 0.0).astype(BF16)

    key_l = lax.broadcasted_iota(I32, (tb, tb), 0)
    query_l = lax.broadcasted_iota(I32, (tb, tb), 1)
    before = key_l < query_l
    row_lo = lax.broadcasted_iota(I32, (LANES, 1), 0) < HALF
    qblk = i + off

    def block(j, carries, diag):
        ks = pl.multiple_of(j * tb, tb)
        heads = range(H_A)
        zs = []
        for h in heads:
            hp, par = divmod(h, 2)
            q2 = q_ref[0, :, hp * LANES:(hp + 1) * LANES]
            zero_q = jnp.zeros_like(q2)
            qh = jnp.where(lo, q2, zero_q) if par == 0 else jnp.where(lo, zero_q, q2)
            zs.append(_nt_dot(kbf[pl.ds(ks, tb), hp * LANES:(hp + 1) * LANES], qh))
        sp_first, his, los, logsig = [], [], [], []
        for h in heads:
            z = zs[h]
            sp = jnp.maximum(z, 0.0) + jnp.log(1.0 + jnp.exp(-jnp.abs(z)))
            logsig.append(z - sp)
            if diag:
                sp = jnp.where(before, sp, 0.0)
            hi = sp.astype(BF16)
            his.append(hi)
            los.append((sp - hi.astype(F32)).astype(BF16))
            sp_first.append(sp[0:1, :])
        sufs = [_dot(tri[...], his[h]) + _dot(tri[...], los[h]) for h in heads]
        weights = []
        for h in heads:
            a = jnp.exp(logsig[h] + (carries[h] - sufs[h]))
            if diag:
                a = jnp.where(before, a, 0.0)
            weights.append(a.astype(BF16))
        for hp in range(n_pairs):
            vtb = vt[j, hp * LANES:(hp + 1) * LANES, :]
            acc_ref[hp] = acc_ref[hp] + jnp.where(row_lo, _dot(vtb, weights[2 * hp]), _dot(vtb, weights[2 * hp + 1]))
        return tuple(carries[h] - (sufs[h][0:1, :] + sp_first[h]) for h in heads)

    def any_weight_left(carries):
        top = carries[0]
        for c in carries[1:]:
            top = jnp.maximum(top, c)
        return (jnp.max(top) > EXP_UNDERFLOW).astype(I32)

    acc_ref[...] = jnp.zeros(acc_ref.shape, F32)
    carries = block(qblk, tuple(jnp.zeros((1, tb), F32) for _ in range(H_A)), True)

    def cond(state):
        return jnp.logical_and(state[0] < qblk, state[1] > 0)

    def body(state):
        t = state[0]
        carries = block(qblk - 1 - t, state[2:], False)
        return (t + 1, any_weight_left(carries)) + carries

    lax.while_loop(cond, body, (jnp.int32(0), any_weight_left(carries)) + carries)
    for hp in range(n_pairs):
        o_ref[0, :, hp * LANES:(hp + 1) * LANES] = acc_ref[hp].T


def _stick_breaking(q, k_new, v_new, k_past, v_past):
    b, t, _ = q.shape
    has_past = k_past is not None
    p = k_past.shape[1] if has_past else 0
    tb = ATT_BLOCK
    assert t % tb == 0 and p % tb == 0
    kv_new = pl.BlockSpec((1, t, W_A), lambda bi, i: (bi, 0, 0))
    kv_past = pl.BlockSpec((1, p, W_A), lambda bi, i: (bi, 0, 0))
    qo = pl.BlockSpec((1, tb, W_A), lambda bi, i: (bi, i, 0))
    ins = [q, k_new, v_new] + ([k_past, v_past] if has_past else [])
    specs = [qo, kv_new, kv_new] + ([kv_past, kv_past] if has_past else [])
    length = p + t
    return pl.pallas_call(
        functools.partial(_sb_kernel, tb=tb, off=p // tb, past_len=p, has_past=has_past),
        grid=(b, t // tb),
        in_specs=specs,
        out_specs=qo,
        out_shape=jax.ShapeDtypeStruct((b, t, W_A), F32),
        scratch_shapes=[pltpu.VMEM((length, W_A), BF16), pltpu.VMEM((length // tb, W_A, tb), BF16),
                        pltpu.VMEM((tb, tb), BF16), pltpu.VMEM((W_A // LANES, LANES, tb), F32)],
        compiler_params=_params(("parallel", "arbitrary")),
        name=f"sb_t{t}",
    )(*ins)


def _t5_bucket(rel):
    half = N_BUCKETS // 2
    max_exact = half // 2
    n = jnp.abs(rel)
    n_f = jnp.maximum(n, 1).astype(F32)
    large = max_exact + (jnp.log(n_f / max_exact) / math.log(MAX_DISTANCE / max_exact)
                         * (half - max_exact)).astype(I32)
    large = jnp.minimum(large, half - 1)
    return jnp.where(rel > 0, half, 0) + jnp.where(n < max_exact, n, large)


def _bias_kernel(rb_ref, o_ref, *, tb):
    h = pl.program_id(0)
    d = pl.program_id(1)
    key = lax.broadcasted_iota(I32, (tb, tb), 0)
    query = lax.broadcasted_iota(I32, (tb, tb), 1)
    bucket = _t5_bucket(key - query - d * tb)
    acc = jnp.zeros((tb, tb), F32)
    for bkt in range(N_BUCKETS):
        acc = jnp.where(bucket == bkt, rb_ref[bkt, h], acc)
    o_ref[0, 0] = acc


def _num_bias_diagonals(tb):
    return -(-MAX_DISTANCE // tb) + 1


def _bias_tiles(rel_bias, tb):
    nd = _num_bias_diagonals(tb)
    return pl.pallas_call(
        functools.partial(_bias_kernel, tb=tb),
        grid=(H_B, nd),
        in_specs=[pl.BlockSpec(memory_space=pltpu.SMEM)],
        out_specs=pl.BlockSpec((1, 1, tb, tb), lambda h, d: (h, d, 0, 0)),
        out_shape=jax.ShapeDtypeStruct((H_B, nd, tb, tb), F32),
        compiler_params=_params(("parallel", "parallel")),
        name=f"bias_tb{tb}",
    )(rel_bias)


def _dsa_kernel(*refs, tb, off, past_len, has_past, n_select, nd, length):
    if has_past:
        (qb_ref, qi_ref, wi_ref, kn_ref, vn_ref, in_ref, kp_ref, vp_ref, ip_ref, bt_ref, far_ref,
         o_ref, k2, vt_lo, vt_hi, i2, sc, sch, scl, mk, s_ref, m_ref, l_ref, acc_ref) = refs
    else:
        (qb_ref, qi_ref, wi_ref, kn_ref, vn_ref, in_ref, bt_ref, far_ref,
         o_ref, k2, vt_lo, vt_hi, i2, sc, sch, scl, mk, s_ref, m_ref, l_ref, acc_ref) = refs
    i = pl.program_id(1)
    lo = _lane_lo()
    qblk = i + off
    nkb = qblk + 1

    @pl.when(i == 0)
    def _fill():
        r = lax.broadcasted_iota(I32, (LANES, LANES), 0)
        c = lax.broadcasted_iota(I32, (LANES, LANES), 1)
        eye_lo = jnp.where((r == c) & (r < HALF), 1.0, 0.0).astype(BF16)
        eye_hi = jnp.where((r == c) & (r >= HALF), 1.0, 0.0).astype(BF16)

        def put(j0, kd, vd, idd):
            n = kd.shape[0]
            k2[j0 * tb:j0 * tb + n, :] = kd
            i2[j0 * tb:j0 * tb + n, :] = idd
            for jb in range(n // tb):
                blk = vd[jb * tb:(jb + 1) * tb, :]
                vt_lo[j0 + jb] = _nt_dot(eye_lo, blk).astype(BF16)
                vt_hi[j0 + jb] = _nt_dot(eye_hi, blk).astype(BF16)

        if has_past:
            rr = lax.broadcasted_iota(I32, (HALF, LANES), 0)
            cc = lax.broadcasted_iota(I32, (HALF, LANES), 1)
            dup = jnp.where((cc == rr) | (cc == rr + HALF), 1.0, 0.0).astype(BF16)
            widen = lambda x: _dot(x.astype(BF16), dup).astype(BF16)
            put(0, widen(kp_ref[0]), widen(vp_ref[0]), widen(ip_ref[0]))
        put(past_len // tb, kn_ref[0], vn_ref[0], in_ref[0])

    key_l = lax.broadcasted_iota(I32, (tb, tb), 0)
    query_l = lax.broadcasted_iota(I32, (tb, tb), 1)
    admissible = (key_l // CHUNK) <= (query_l // CHUNK)

    def head_queries(ref, hp):
        q2 = ref[0, :, hp * LANES:(hp + 1) * LANES]
        zero = jnp.zeros_like(q2)
        return jnp.where(lo, q2, zero), jnp.where(lo, zero, q2)

    wi_t = wi_ref[0].T

    def score_rows(j0, n_blk, diag_last):
        ks = pl.multiple_of(j0 * tb, tb)
        kib = i2[pl.ds(ks, n_blk * tb), :]
        dots = []
        for hp in range(H_IDX // 2):
            for qm in head_queries(qi_ref, hp):
                dots.append(_nt_dot(kib, qm))
        s = jnp.zeros((n_blk * tb, tb), F32)
        for h in range(H_IDX):
            s = s + wi_t[h:h + 1, :] * jnp.maximum(dots[h], 0.0)
        bits = pltpu.bitcast(s, I32)
        keys = bits ^ ((bits >> 31) & 0x7FFFFFFF)
        for b in range(n_blk):
            key = keys[b * tb:(b + 1) * tb, :]
            if diag_last and b == n_blk - 1:
                key = jnp.where(admissible, key, INT_MIN)
            sc[j0 + b] = key
            sch[j0 + b] = (key >> 16).astype(I16)
            scl[j0 + b] = ((key & 0xFFFF) + I16_MIN).astype(I16)

    def score_pair(t, _):
        score_rows(2 * t, 2, False)
        return 0

    lax.fori_loop(0, qblk // 2, score_pair, 0)

    @pl.when(qblk % 2 == 1)
    def _score_tail_pair():
        score_rows(qblk - 1, 2, True)

    @pl.when(qblk % 2 == 0)
    def _score_tail_single():
        score_rows(qblk, 1, True)

    kf = float(n_select)

    def count(pred_fn):
        def body(j, c):
            hit = pred_fn(sc[j], j * tb)
            ones = jnp.where(hit, 1.0, 0.0)
            return c + jnp.sum(ones.reshape(tb // SUBLANES, SUBLANES, tb), axis=0)
        c8 = lax.fori_loop(0, nkb, body, jnp.zeros((SUBLANES, tb), F32))
        return jnp.sum(c8, axis=0, keepdims=True)

    def threshold_of(n_blocks):
        def count16(ref, pred_fn):
            parts = []
            for j in range(n_blocks):
                ones = jnp.where(pred_fn(ref[j]), jnp.ones((tb, tb), I16), jnp.zeros((tb, tb), I16))
                parts += [ones[g * PACKED_ROWS:(g + 1) * PACKED_ROWS, :] for g in range(tb // PACKED_ROWS)]
            while len(parts) > 1:
                parts = [a + b for a, b in zip(parts[0::2], parts[1::2])] + ([parts[-1]] if len(parts) % 2 else [])
            return jnp.sum(parts[0].astype(F32), axis=0, keepdims=True)

        def search16(ref, base):
            def bit_body(t, tau):
                cand = tau + jnp.left_shift(jnp.int32(1), 15 - t)
                cand16 = cand.astype(I16)
                c = base + count16(ref, lambda blk: blk >= cand16)
                return jnp.where(c >= kf, cand, tau)
            return lax.fori_loop(0, 16, bit_body, jnp.full((1, tb), I16_MIN, I32))

        tau_hi = search16(sch, 0.0)
        tau_hi16 = tau_hi.astype(I16)
        above = count16(sch, lambda blk: blk > tau_hi16)
        for j in range(n_blocks):
            scl[j] = jnp.where(sch[j] == tau_hi16, scl[j], jnp.full((tb, tb), I16_MIN, I16))
        tau_lo = search16(scl, above)
        return tau_hi * 65536 + (tau_lo - I16_MIN)

    tau = lax.switch(i, [functools.partial(threshold_of, n) for n in range(off + 1, length // tb + 1)])
    cnt_ge = count(lambda blk, ks: blk >= tau)
    has_thr = tau > INT_MIN
    tie = has_thr & (cnt_ge > kf)
    any_tie = jnp.max(jnp.where(tie, 1.0, 0.0)) > 0.0

    def write_mask(sel_fn):
        def body(j, _):
            mk[j] = jnp.where(sel_fn(sc[j], j * tb), 0.0, NEG)
            return 0
        lax.fori_loop(0, nkb, body, 0)

    @pl.when(jnp.logical_not(any_tie))
    def _plain():
        thr = jnp.where(has_thr, tau, INT_MIN + 1)
        write_mask(lambda blk, ks: blk >= thr)

    @pl.when(any_tie)
    def _ties():
        need = kf - count(lambda blk, ks: blk > tau)
        q_idx = jnp.zeros((1, tb), I32)
        for bit in reversed(range(max(1, (length - 1).bit_length()))):
            cand = q_idx + (1 << bit)
            c = count(lambda blk, ks: (blk == tau) & ((key_l + ks) < cand))
            q_idx = jnp.where(c < need, cand, q_idx)
        last_eq = jnp.where(has_thr, jnp.where(tie, q_idx, length), -1)
        write_mask(lambda blk, ks: (blk > tau) | ((blk == tau) & ((key_l + ks) <= last_eq)))

    m_ref[...] = jnp.full(m_ref.shape, NEG, F32)
    l_ref[...] = jnp.zeros(l_ref.shape, F32)
    acc_ref[...] = jnp.zeros(acc_ref.shape, F32)
    row_lo = lax.broadcasted_iota(I32, (LANES, 1), 0) < HALF
    n_far = jnp.maximum(qblk - nd + 1, 0)

    def logits_rows(j0, ds):
        n_blk = len(ds)
        ks = pl.multiple_of(j0 * tb, tb)
        kblk = k2[pl.ds(ks, n_blk * tb), :]
        for hp in range(H_B // 2):
            for par, qm in enumerate(head_queries(qb_ref, hp)):
                h = 2 * hp + par
                rows = _nt_dot(kblk, qm)
                for b, d in enumerate(ds):
                    s = rows[b * tb:(b + 1) * tb, :] + mk[j0 + b]
                    if d is not None:
                        s = s + bt_ref[h, d]
                    s_ref[h, j0 + b] = s
                    top = jnp.max(s, axis=0, keepdims=True)
                    if d is None:
                        top = top + far_ref[h]
                    m_ref[h] = jnp.maximum(m_ref[h], top)

    def weigh_rows(j0, fars):
        ps = [[] for _ in fars]
        for h in range(H_B):
            for b, far in enumerate(fars):
                shift = m_ref[h] - far_ref[h] if far else m_ref[h]
                p = jnp.exp(s_ref[h, j0 + b] - shift)
                l_ref[h] = l_ref[h] + jnp.sum(p, axis=0, keepdims=True)
                ps[b].append(p.astype(BF16))
        for hp in range(H_B // 2):
            pv = None
            for b in range(len(fars)):
                part = _dot(vt_lo[j0 + b], ps[b][2 * hp]) + _dot(vt_hi[j0 + b], ps[b][2 * hp + 1])
                pv = part if pv is None else pv + part
            acc_ref[hp] = acc_ref[hp] + pv

    assert nd == 2

    def far_logits_pair(t, _):
        logits_rows(2 * t, (None, None))
        return 0

    lax.fori_loop(0, n_far // 2, far_logits_pair, 0)

    @pl.when(n_far % 2 == 1)
    def _far_logits_single():
        logits_rows(n_far - 1, (None,))

    @pl.when(qblk >= 1)
    def _near_logits_pair():
        logits_rows(qblk - 1, (1, 0))

    @pl.when(qblk == 0)
    def _near_logits_single():
        logits_rows(qblk, (0,))

    def far_weigh_pair(t, _):
        weigh_rows(2 * t, (True, True))
        return 0

    lax.fori_loop(0, n_far // 2, far_weigh_pair, 0)

    @pl.when(n_far % 2 == 1)
    def _far_weigh_single():
        weigh_rows(n_far - 1, (True,))

    @pl.when(qblk >= 1)
    def _near_weigh_pair():
        weigh_rows(qblk - 1, (False, False))

    @pl.when(qblk == 0)
    def _near_weigh_single():
        weigh_rows(qblk, (False,))

    for hp in range(H_B // 2):
        denom = jnp.where(row_lo, l_ref[2 * hp], l_ref[2 * hp + 1])
        o_ref[0, :, hp * LANES:(hp + 1) * LANES] = (acc_ref[hp] / denom).T


def _sparse_attention(q_b, q_i, w_i, k2_new, v2_new, i2_new, k_past, v_past, i_past, bias_tiles, bias_far,
                      n_select):
    b, t, _ = q_b.shape
    has_past = k_past is not None
    p = k_past.shape[1] if has_past else 0
    tb = ATT_BLOCK
    assert t % tb == 0 and p % tb == 0 and tb % CHUNK == 0 and bias_tiles.shape[2] == tb
    length = p + t
    nd = bias_tiles.shape[1]
    nb = length // tb
    qspec = pl.BlockSpec((1, tb, W_B), lambda bi, i: (bi, i, 0))
    new = pl.BlockSpec((1, t, LANES), lambda bi, i: (bi, 0, 0))
    past = pl.BlockSpec((1, p, HD_B), lambda bi, i: (bi, 0, 0))
    ins = [q_b, q_i, w_i, k2_new, v2_new, i2_new] + ([k_past, v_past, i_past] if has_past else [])
    specs = ([qspec, qspec, pl.BlockSpec((1, tb, LANES), lambda bi, i: (bi, i, 0)), new, new, new]
             + ([past, past, past] if has_past else []))
    ins += [bias_tiles, bias_far]
    specs += [pl.BlockSpec(bias_tiles.shape, lambda bi, i: (0, 0, 0, 0)), pl.BlockSpec(memory_space=pltpu.SMEM)]
    return pl.pallas_call(
        functools.partial(_dsa_kernel, tb=tb, off=p // tb, past_len=p, has_past=has_past,
                          n_select=n_select, nd=nd, length=length),
        grid=(b, t // tb),
        in_specs=specs,
        out_specs=qspec,
        out_shape=jax.ShapeDtypeStruct((b, t, W_B), F32),
        scratch_shapes=[pltpu.VMEM((length, LANES), BF16),
                        pltpu.VMEM((nb, LANES, tb), BF16), pltpu.VMEM((nb, LANES, tb), BF16),
                        pltpu.VMEM((length, LANES), BF16),
                        pltpu.VMEM((nb, tb, tb), I32),
                        pltpu.VMEM((nb, tb, tb), I16), pltpu.VMEM((nb, tb, tb), I16),
                        pltpu.VMEM((nb, tb, tb), F32),
                        pltpu.VMEM((H_B, nb, tb, tb), F32),
                        pltpu.VMEM((H_B, 1, tb), F32), pltpu.VMEM((H_B, 1, tb), F32),
                        pltpu.VMEM((H_B // 2, LANES, tb), F32)],
        compiler_params=_params(("parallel", "arbitrary")),
        name=f"dsa_t{t}",
    )(*ins)


def _sigmoid(x):
    return 1.0 / (1.0 + jnp.exp(-x))


def _merge_kernel(x_ref, ya_ref, yb_ref, wg_ref, bg_ref, wpa_ref, wpb_ref, wo_ref, g_ref, b_ref, o_ref):
    x = x_ref[...]
    xb = x.astype(BF16)

    def seg(c0, n):
        return _dot(xb, wg_ref[:, c0:c0 + n]) + bg_ref[:, c0:c0 + n]

    g_a = seg(C_GA, W_A)
    y_a = (ya_ref[...] * (g_a * _sigmoid(g_a))).astype(BF16)
    branch_a = _sigmoid(seg(C_RA, D_MODEL)) * _dot(y_a, wpa_ref[...])
    g_b = seg(C_GB, W_B)
    y_b = (yb_ref[...] * (g_b * _sigmoid(g_b))).astype(BF16)
    branch_b = _sigmoid(seg(C_RB, D_MODEL)) * _dot(y_b, wpb_ref[...])
    merged = (branch_a + branch_b).astype(BF16)
    o_ref[...] = _ln(ALPHA * x + _dot(merged, wo_ref[...]), g_ref[...], b_ref[...])


def _merge(x2d, y_a, y_b, w_gate, b_gate, w_pa, w_pb, w_out, ln_g, ln_b):
    n = x2d.shape[0]
    tm = min(TOKEN_TILE, n)
    row = lambda width: pl.BlockSpec((tm, width), lambda i: (i, 0))
    full = lambda a: pl.BlockSpec(a.shape, lambda i: (0, 0))
    consts = [w_gate, b_gate, w_pa, w_pb, w_out, ln_g, ln_b]
    return pl.pallas_call(
        _merge_kernel,
        grid=(n // tm,),
        in_specs=[row(D_MODEL), row(W_A), row(W_B)] + [full(a) for a in consts],
        out_specs=row(D_MODEL),
        out_shape=jax.ShapeDtypeStruct((n, D_MODEL), F32),
        compiler_params=_params(("parallel",)),
        name=f"merge_n{n}",
    )(x2d, y_a, y_b, *consts)


def _pack_weights(w_in, b_in):
    offs = np.concatenate([[0], np.cumsum(SPLIT_SIZES)])
    names = ("q_a", "k_a", "v_a", "g_a", "q_b", "k_b", "v_b", "g_b", "q_i", "k_i", "w_i", "r_a", "r_b")
    w = {nm: w_in[:, :, offs[k]:offs[k + 1]] for k, nm in enumerate(names)}
    b = {nm: b_in[:, offs[k]:offs[k + 1]] for k, nm in enumerate(names)}
    pad = LANES - H_IDX

    def build(parts, last):
        order = (parts["q_a"] * SB_SCALE, parts["k_a"], parts["v_a"], parts["q_b"] * ATT_SCALE, parts["q_i"],
                 parts["k_b"], parts["k_b"], parts["v_b"], parts["v_b"], parts["k_i"], parts["k_i"], last)
        return jnp.concatenate(order, axis=-1)

    w_att = build(w, jnp.pad(w["w_i"], ((0, 0), (0, 0), (0, pad)))).astype(BF16)
    b_att = build(b, jnp.pad(b["w_i"], ((0, 0), (0, pad))))[:, None, :]
    w_gate = jnp.concatenate([w["g_a"], w["g_b"], w["r_a"], w["r_b"]], axis=-1).astype(BF16)
    b_gate = jnp.concatenate([b["g_a"], b["g_b"], b["r_a"], b["r_b"]], axis=-1)[:, None, :]
    return w_att, b_att, w_gate, b_gate


def _trunk_layer(x2d, batch, seq, past, n_select, weights, bias_tiles, bias_far, ln_in=None):
    w_att, b_att, w_gate, b_gate, w_pa, w_pb, w_out, ln_g, ln_b = weights
    (q_a, q_b, q_i, kb2, vb2, ki2, w_i), new_rows, x2d = _project(x2d, w_att, b_att, ln_in)
    shape3 = lambda a: a.reshape(batch, seq, a.shape[-1])
    if past is None:
        pk_a = pv_a = pk_b = pv_b = pk_i = None
    else:
        pk_a, pv_a, pk_b, pv_b, pk_i = past
        pk_a = pk_a.reshape(batch, -1, W_A)
        pv_a = pv_a.reshape(batch, -1, W_A)
    y_a = _stick_breaking(shape3(q_a), shape3(new_rows[0]), shape3(new_rows[1]), pk_a, pv_a)
    y_b = _sparse_attention(shape3(q_b), shape3(q_i), shape3(w_i), shape3(kb2), shape3(vb2), shape3(ki2),
                            pk_b, pv_b, pk_i, bias_tiles, bias_far, n_select)
    x_next = _merge(x2d, y_a.reshape(-1, W_A), y_b.reshape(-1, W_B),
                    w_gate, b_gate, w_pa, w_pb, w_out, ln_g, ln_b)
    return x_next, new_rows


def kernel(x_prompt, x_sample, cache_sb_k, cache_sb_v, cache_dsa_k, cache_dsa_v, cache_idx_k,
           ln_in_g, ln_in_b, w_in, b_in, w_proj_a, w_proj_b, w_out, ln_g, ln_b, rel_bias):
    batch, seq, _ = x_prompt.shape
    dec_batch, dec_seq, _ = x_sample.shape
    past_len = cache_sb_k.shape[2]
    dec_pad = -(-dec_seq // ATT_BLOCK) * ATT_BLOCK
    n_sel_prompt = min(MAX_SELECT, seq // 4)
    n_sel_sample = min(MAX_SELECT, (past_len + dec_seq) // 4)
    w_att, b_att, w_gate, b_gate = _pack_weights(w_in, b_in)
    w_pa = w_proj_a.astype(BF16)
    w_pb = w_proj_b.astype(BF16)
    w_o = w_out.astype(BF16)
    bias_far = rel_bias[N_BUCKETS // 2 - 1]
    tiles = _bias_tiles(rel_bias, ATT_BLOCK)
    hp = x_prompt.reshape(-1, D_MODEL)
    hs = jnp.pad(x_sample, ((0, 0), (0, dec_pad - dec_seq), (0, 0))).reshape(-1, D_MODEL)
    rows_p, rows_s = [], []
    for layer in range(DEPTH):
        weights = (w_att[layer], b_att[layer], w_gate[layer], b_gate[layer], w_pa[layer], w_pb[layer], w_o[layer],
                   ln_g[layer].reshape(1, D_MODEL), ln_b[layer].reshape(1, D_MODEL))
        ln_in = (ln_in_g, ln_in_b) if layer == 0 else None
        hp, new_p = _trunk_layer(hp, batch, seq, None, n_sel_prompt, weights, tiles, bias_far, ln_in)
        past = (cache_sb_k[layer], cache_sb_v[layer], cache_dsa_k[layer], cache_dsa_v[layer], cache_idx_k[layer])
        hs, new_s = _trunk_layer(hs, dec_batch, dec_pad, past, n_sel_sample, weights, tiles, bias_far, ln_in)
        rows_p.append(new_p)
        rows_s.append(new_s)

    def shaped(stacked, b, t_pad, t):
        k_a, v_a, k_b, v_b, k_i = stacked
        heads = lambda a: a.reshape(DEPTH, b, t_pad, H_A, HD_A)[:, :, :t]
        flat = lambda a: a.reshape(DEPTH, b, t_pad, a.shape[-1])[:, :, :t]
        return heads(k_a), heads(v_a), flat(k_b), flat(v_b), flat(k_i)

    y_s = hs.reshape(dec_batch, dec_pad, D_MODEL)[:, :dec_seq]
    return ((hp.reshape(batch, seq, D_MODEL), y_s) + shaped(_collect(rows_p), batch, seq, seq)
            + shaped(_collect(rows_s), dec_batch, dec_pad, dec_seq))
```

```python
import functools
import math

import jax
import jax.numpy as jnp
import numpy as np
from jax import lax
from jax.experimental import pallas as pl
from jax.experimental.pallas import tpu as pltpu

F32 = jnp.float32
BF16 = jnp.bfloat16
I32 = jnp.int32
I16 = jnp.int16

D_MODEL = 1024
DEPTH = 4
CHUNK = 64
H_A = 8
HD_A = 64
W_A = H_A * HD_A
H_B = 8
HD_B = 64
W_B = H_B * HD_B
H_IDX = 8
D_IDX = 64
MAX_SELECT = 256
N_BUCKETS = 32
MAX_DISTANCE = 128
LN_EPS = 1e-5
ALPHA = (2 * DEPTH) ** 0.25
SB_SCALE = HD_A ** -0.5
ATT_SCALE = HD_B ** -0.5
SPLIT_SIZES = (W_A, W_A, W_A, W_A, W_B, HD_B, HD_B, W_B, H_IDX * D_IDX, D_IDX, H_IDX, D_MODEL, D_MODEL)

LANES = 128
SUBLANES = 8
PACKED_ROWS = 2 * SUBLANES
HALF = 64
NEG = -1e30
INT_MIN = -(2 ** 31)
I16_MIN = -(2 ** 15)
LOG2E = math.log2(math.e)
EXP2_UNDERFLOW = -152.0
V7X_VMEM_LIMIT = 56 * 1024 * 1024
ATT_BLOCK = 256
GROUP = 4
TOKEN_TILE = 1024

C_QA, C_KA, C_VA, C_QB, C_QI = 0, 512, 1024, 1536, 2048
C_KB2, C_VB2, C_KI2, C_WI = 2560, 2688, 2816, 2944
N_ATT = 3072
C_GA, C_GB, C_RA, C_RB = 0, 512, 1024, 2048
N_GATE = 3072


def _params(sem):
    return pltpu.CompilerParams(dimension_semantics=sem, vmem_limit_bytes=V7X_VMEM_LIMIT)


def _nt_dot(a, b):
    return lax.dot_general(a, b, (((1,), (1,)), ((), ())), preferred_element_type=F32)


def _dot(a, b):
    return jnp.dot(a, b, preferred_element_type=F32)


def _lane_lo():
    return lax.broadcasted_iota(I32, (1, LANES), 1) < HALF


def _ln(x, g, b):
    mu = jnp.mean(x, axis=-1, keepdims=True)
    xc = x - mu
    var = jnp.mean(xc * xc, axis=-1, keepdims=True)
    return xc * lax.rsqrt(var + LN_EPS) * g + b


NEW_ROW_WIDTHS = (W_A, W_A, HD_B, HD_B, D_IDX)


def _proj_kernel(*refs, norm_input):
    if norm_input:
        x_ref, g_ref, beta_ref, w_ref, b_ref, *outs, h_ref = refs
        x = _ln(x_ref[...], g_ref[...], beta_ref[...])
        h_ref[...] = x
    else:
        x_ref, w_ref, b_ref, *outs = refs
        x = x_ref[...]
    (qa_ref, qb_ref, qi_ref, kb2_ref, vb2_ref, ki2_ref, wi_ref, ka_ref, va_ref, kb_ref, vb_ref, ki_ref) = outs
    xb = x.astype(BF16)

    def seg(c0, n):
        return _dot(xb, w_ref[:, c0:c0 + n]) + b_ref[:, c0:c0 + n]

    qa_ref[...] = seg(C_QA, W_A).astype(BF16)
    ka_ref[...] = seg(C_KA, W_A)
    va_ref[...] = seg(C_VA, W_A)
    qb_ref[...] = seg(C_QB, W_B).astype(BF16)
    qi_ref[...] = seg(C_QI, H_IDX * D_IDX).astype(BF16)
    kb = seg(C_KB2, LANES)
    kb2_ref[...] = kb.astype(BF16)
    kb_ref[...] = kb[:, :HD_B]
    vb = seg(C_VB2, LANES)
    vb2_ref[...] = vb.astype(BF16)
    vb_ref[...] = vb[:, :HD_B]
    ki = seg(C_KI2, LANES)
    ki2_ref[...] = ki.astype(BF16)
    ki_ref[...] = ki[:, :D_IDX]
    wi_ref[...] = seg(C_WI, LANES)


def _project(x2d, w_att, b_att, ln_in=None):
    n = x2d.shape[0]
    tm = min(TOKEN_TILE, n)
    row = lambda width: pl.BlockSpec((tm, width), lambda i: (i, 0))
    full = lambda a: pl.BlockSpec(a.shape, lambda i: (0, 0))
    sds = lambda width, dt: jax.ShapeDtypeStruct((n, width), dt)
    n_plain = 7
    norm_input = ln_in is not None
    ln_args = [a.reshape(1, D_MODEL) for a in ln_in] if norm_input else []
    outs = pl.pallas_call(
        functools.partial(_proj_kernel, norm_input=norm_input),
        grid=(n // tm,),
        in_specs=[row(D_MODEL)] + [full(a) for a in ln_args] + [full(w_att), full(b_att)],
        out_specs=[row(W_A), row(W_B), row(H_IDX * D_IDX), row(LANES), row(LANES), row(LANES), row(LANES)]
                  + [row(width) for width in NEW_ROW_WIDTHS] + ([row(D_MODEL)] if norm_input else []),
        out_shape=[sds(W_A, BF16), sds(W_B, BF16), sds(H_IDX * D_IDX, BF16),
                   sds(LANES, BF16), sds(LANES, BF16), sds(LANES, BF16), sds(LANES, F32)]
                  + [sds(width, F32) for width in NEW_ROW_WIDTHS] + ([sds(D_MODEL, F32)] if norm_input else []),
        compiler_params=_params(("parallel",)),
        name=f"proj_n{n}",
    )(x2d, *ln_args, w_att, b_att)
    n_rows = len(NEW_ROW_WIDTHS)
    return outs[:n_plain], tuple(outs[n_plain:n_plain + n_rows]), (outs[-1] if norm_input else x2d)


def _collect_kernel(*refs):
    ins, (ka_o, va_o, kb_o, vb_o, ki_o) = refs[:-5], refs[-5:]
    for layer in range(DEPTH):
        ka, va, kb, vb, ki = ins[5 * layer:5 * layer + 5]
        tm = ka.shape[0]
        for src, dst in ((ka, ka_o), (va, va_o)):
            rows = src[...]
            for h in range(H_A):
                dst[layer, pl.ds(h, tm, stride=H_A), :] = rows[:, h * HD_A:(h + 1) * HD_A]
        kb_o[layer] = kb[...]
        vb_o[layer] = vb[...]
        ki_o[layer] = ki[...]


COLLECT_TILE = 256


def _collect(rows_per_layer):
    n = rows_per_layer[0][0].shape[0]
    tm = min(COLLECT_TILE, n)
    flat = [a for rows in rows_per_layer for a in rows]
    in_specs = [pl.BlockSpec((tm, a.shape[1]), lambda i: (i, 0)) for a in flat]
    heads = pl.BlockSpec((DEPTH, tm * H_A, HD_A), lambda i: (0, i, 0))
    narrow = lambda width: pl.BlockSpec((DEPTH, tm, width), lambda i: (0, i, 0))
    return pl.pallas_call(
        _collect_kernel,
        grid=(n // tm,),
        in_specs=in_specs,
        out_specs=[heads, heads, narrow(HD_B), narrow(HD_B), narrow(D_IDX)],
        out_shape=[jax.ShapeDtypeStruct((DEPTH, n * H_A, HD_A), F32), jax.ShapeDtypeStruct((DEPTH, n * H_A, HD_A), F32),
                   jax.ShapeDtypeStruct((DEPTH, n, HD_B), F32), jax.ShapeDtypeStruct((DEPTH, n, HD_B), F32),
                   jax.ShapeDtypeStruct((DEPTH, n, D_IDX), F32)],
        compiler_params=_params(("parallel",)),
        name=f"collect_n{n}",
    )(*flat)


def _sb_kernel(*refs, tb, off, past_len, has_past):
    if has_past:
        q_ref, kn_ref, vn_ref, kp_ref, vp_ref, o_ref, kbf, vt, tri, acc_ref = refs
    else:
        q_ref, kn_ref, vn_ref, o_ref, kbf, vt, tri, acc_ref = refs
    i = pl.program_id(1)
    lo = _lane_lo()
    n_pairs = W_A // LANES

    @pl.when(i == 0)
    def _fill():
        r = lax.broadcasted_iota(I32, (LANES, LANES), 0)
        c = lax.broadcasted_iota(I32, (LANES, LANES), 1)
        eye = jnp.where(r == c, 1.0, 0.0).astype(BF16)

        def put(j0, k, v):
            n = k.shape[0]
            kbf[j0 * tb:j0 * tb + n, :] = k.astype(BF16)
            for jb in range(n // tb):
                for hp in range(n_pairs):
                    blk = v[jb * tb:(jb + 1) * tb, hp * LANES:(hp + 1) * LANES].astype(BF16)
                    vt[j0 + jb, hp * LANES:(hp + 1) * LANES, :] = _nt_dot(eye, blk).astype(BF16)

        if has_past:
            put(0, kp_ref[0], vp_ref[0])
        put(past_len // tb, kn_ref[0], vn_ref[0])
        rr = lax.broadcasted_iota(I32, (tb, tb), 0)
        cc = lax.broadcasted_iota(I32, (tb, tb), 1)
        tri[...] = jnp.where(cc > rr, 1.0, 0.0).astype(BF16)

    key_l = lax.broadcasted_iota(I32, (tb, tb), 0)
    query_l = lax.broadcasted_iota(I32, (tb, tb), 1)
    before = key_l < query_l
    row_lo = lax.broadcasted_iota(I32, (LANES, 1), 0) < HALF
    qblk = i + off

    def block(j, carries, diag):
        ks = pl.multiple_of(j * tb, tb)
        heads = range(H_A)
        zs = []
        for h in heads:
            hp, par = divmod(h, 2)
            q2 = q_ref[0, :, hp * LANES:(hp + 1) * LANES]
            zero_q = jnp.zeros_like(q2)
            qh = jnp.where(lo, q2, zero_q) if par == 0 else jnp.where(lo, zero_q, q2)
            zs.append(_nt_dot(kbf[pl.ds(ks, tb), hp * LANES:(hp + 1) * LANES], qh))
        sp_first, his, los, logsig = [], [], [], []
        for h in heads:
            z = zs[h]
            sp = jnp.maximum(z, 0.0) + jnp.log2(1.0 + jnp.exp2(-jnp.abs(z)))
            logsig.append(z - sp)
            if diag:
                sp = jnp.where(before, sp, 0.0)
            hi = sp.astype(BF16)
            his.append(hi)
            los.append((sp - hi.astype(F32)).astype(BF16))
            sp_first.append(sp[0:1, :])
        sufs = [_dot(tri[...], his[h]) + _dot(tri[...], los[h]) for h in heads]
        weights = []
        for h in heads:
            a = jnp.exp2(logsig[h] + (carries[h] - sufs[h]))
            if diag:
                a = jnp.where(before, a, 0.0)
            weights.append(a.astype(BF16))
        for hp in range(n_pairs):
            vtb = vt[j, hp * LANES:(hp + 1) * LANES, :]
            acc_ref[hp] = acc_ref[hp] + jnp.where(row_lo, _dot(vtb, weights[2 * hp]), _dot(vtb, weights[2 * hp + 1]))
        return tuple(carries[h] - (sufs[h][0:1, :] + sp_first[h]) for h in heads)

    def any_weight_left(carries):
        top = carries[0]
        for c in carries[1:]:
            top = jnp.maximum(top, c)
        return (jnp.max(top) > EXP2_UNDERFLOW).astype(I32)

    acc_ref[...] = jnp.zeros(acc_ref.shape, F32)
    carries = block(qblk, tuple(jnp.zeros((1, tb), F32) for _ in range(H_A)), True)

    def cond(state):
        return jnp.logical_and(state[0] < qblk, state[1] > 0)

    def body(state):
        t = state[0]
        carries = block(qblk - 1 - t, state[2:], False)
        return (t + 1, any_weight_left(carries)) + carries

    lax.while_loop(cond, body, (jnp.int32(0), any_weight_left(carries)) + carries)
    for hp in range(n_pairs):
        o_ref[0, :, hp * LANES:(hp + 1) * LANES] = acc_ref[hp].T


def _stick_breaking(q, k_new, v_new, k_past, v_past):
    b, t, _ = q.shape
    has_past = k_past is not None
    p = k_past.shape[1] if has_past else 0
    tb = ATT_BLOCK
    assert t % tb == 0 and p % tb == 0
    kv_new = pl.BlockSpec((1, t, W_A), lambda bi, i: (bi, 0, 0))
    kv_past = pl.BlockSpec((1, p, W_A), lambda bi, i: (bi, 0, 0))
    qo = pl.BlockSpec((1, tb, W_A), lambda bi, i: (bi, i, 0))
    ins = [q, k_new, v_new] + ([k_past, v_past] if has_past else [])
    specs = [qo, kv_new, kv_new] + ([kv_past, kv_past] if has_past else [])
    length = p + t
    return pl.pallas_call(
        functools.partial(_sb_kernel, tb=tb, off=p // tb, past_len=p, has_past=has_past),
        grid=(b, t // tb),
        in_specs=specs,
        out_specs=qo,
        out_shape=jax.ShapeDtypeStruct((b, t, W_A), F32),
        scratch_shapes=[pltpu.VMEM((length, W_A), BF16), pltpu.VMEM((length // tb, W_A, tb), BF16),
                        pltpu.VMEM((tb, tb), BF16), pltpu.VMEM((W_A // LANES, LANES, tb), F32)],
        compiler_params=_params(("parallel", "arbitrary")),
        name=f"sb_t{t}",
    )(*ins)


def _t5_bucket(rel):
    half = N_BUCKETS // 2
    max_exact = half // 2
    n = jnp.abs(rel)
    n_f = jnp.maximum(n, 1).astype(F32)
    large = max_exact + (jnp.log(n_f / max_exact) / math.log(MAX_DISTANCE / max_exact)
                         * (half - max_exact)).astype(I32)
    large = jnp.minimum(large, half - 1)
    return jnp.where(rel > 0, half, 0) + jnp.where(n < max_exact, n, large)


def _bias_kernel(rb_ref, o_ref, *, tb):
    h = pl.program_id(0)
    d = pl.program_id(1)
    key = lax.broadcasted_iota(I32, (tb, tb), 0)
    query = lax.broadcasted_iota(I32, (tb, tb), 1)
    bucket = _t5_bucket(key - query - d * tb)
    acc = jnp.zeros((tb, tb), F32)
    for bkt in range(N_BUCKETS):
        acc = jnp.where(bucket == bkt, rb_ref[bkt, h], acc)
    o_ref[0, 0] = acc * LOG2E


def _num_bias_diagonals(tb):
    return -(-MAX_DISTANCE // tb) + 1


def _bias_tiles(rel_bias, tb):
    nd = _num_bias_diagonals(tb)
    return pl.pallas_call(
        functools.partial(_bias_kernel, tb=tb),
        grid=(H_B, nd),
        in_specs=[pl.BlockSpec(memory_space=pltpu.SMEM)],
        out_specs=pl.BlockSpec((1, 1, tb, tb), lambda h, d: (h, d, 0, 0)),
        out_shape=jax.ShapeDtypeStruct((H_B, nd, tb, tb), F32),
        compiler_params=_params(("parallel", "parallel")),
        name=f"bias_tb{tb}",
    )(rel_bias)


def _dsa_kernel(*refs, tb, off, past_len, has_past, n_select, nd, length):
    if has_past:
        (qb_ref, qi_ref, wi_ref, kn_ref, vn_ref, in_ref, kp_ref, vp_ref, ip_ref, bt_ref, far_ref,
         o_ref, k2, vt_lo, vt_hi, i2, sc, sch, scl, mk, s_ref, m_ref, l_ref, acc_ref) = refs
    else:
        (qb_ref, qi_ref, wi_ref, kn_ref, vn_ref, in_ref, bt_ref, far_ref,
         o_ref, k2, vt_lo, vt_hi, i2, sc, sch, scl, mk, s_ref, m_ref, l_ref, acc_ref) = refs
    i = pl.program_id(1)
    lo = _lane_lo()
    qblk = i + off
    nkb = qblk + 1

    @pl.when(i == 0)
    def _fill():
        r = lax.broadcasted_iota(I32, (LANES, LANES), 0)
        c = lax.broadcasted_iota(I32, (LANES, LANES), 1)
        eye_lo = jnp.where((r == c) & (r < HALF), 1.0, 0.0).astype(BF16)
        eye_hi = jnp.where((r == c) & (r >= HALF), 1.0, 0.0).astype(BF16)

        def put(j0, kd, vd, idd):
            n = kd.shape[0]
            k2[j0 * tb:j0 * tb + n, :] = kd
            i2[j0 * tb:j0 * tb + n, :] = idd
            for jb in range(n // tb):
                blk = vd[jb * tb:(jb + 1) * tb, :]
                vt_lo[j0 + jb] = _nt_dot(eye_lo, blk).astype(BF16)
                vt_hi[j0 + jb] = _nt_dot(eye_hi, blk).astype(BF16)

        if has_past:
            rr = lax.broadcasted_iota(I32, (HALF, LANES), 0)
            cc = lax.broadcasted_iota(I32, (HALF, LANES), 1)
            dup = jnp.where((cc == rr) | (cc == rr + HALF), 1.0, 0.0).astype(BF16)
            widen = lambda x: _dot(x.astype(BF16), dup).astype(BF16)
            put(0, widen(kp_ref[0]), widen(vp_ref[0]), widen(ip_ref[0]))
        put(past_len // tb, kn_ref[0], vn_ref[0], in_ref[0])

    key_l = lax.broadcasted_iota(I32, (tb, tb), 0)
    query_l = lax.broadcasted_iota(I32, (tb, tb), 1)
    admissible = (key_l // CHUNK) <= (query_l // CHUNK)

    def head_queries(ref, hp):
        q2 = ref[0, :, hp * LANES:(hp + 1) * LANES]
        zero = jnp.zeros_like(q2)
        return jnp.where(lo, q2, zero), jnp.where(lo, zero, q2)

    wi_t = wi_ref[0].T

    def score_rows(j0, n_blk, diag_last):
        ks = pl.multiple_of(j0 * tb, tb)
        kib = i2[pl.ds(ks, n_blk * tb), :]
        dots = []
        for hp in range(H_IDX // 2):
            for qm in head_queries(qi_ref, hp):
                dots.append(_nt_dot(kib, qm))
        s = jnp.zeros((n_blk * tb, tb), F32)
        for h in range(H_IDX):
            s = s + wi_t[h:h + 1, :] * jnp.maximum(dots[h], 0.0)
        bits = pltpu.bitcast(s, I32)
        keys = bits ^ ((bits >> 31) & 0x7FFFFFFF)
        for b in range(n_blk):
            key = keys[b * tb:(b + 1) * tb, :]
            if diag_last and b == n_blk - 1:
                key = jnp.where(admissible, key, INT_MIN)
            sc[j0 + b] = key
            sch[j0 + b] = (key >> 16).astype(I16)
            scl[j0 + b] = ((key & 0xFFFF) + I16_MIN).astype(I16)

    def score_group(t, _):
        score_rows(GROUP * t, GROUP, False)
        return 0

    lax.fori_loop(0, qblk // GROUP, score_group, 0)
    for rest in range(GROUP):
        @pl.when(qblk % GROUP == rest)
        def _score_tail(rest=rest):
            score_rows(qblk - rest, rest + 1, True)

    kf = float(n_select)

    def count(pred_fn):
        def body(j, c):
            hit = pred_fn(sc[j], j * tb)
            ones = jnp.where(hit, 1.0, 0.0)
            return c + jnp.sum(ones.reshape(tb // SUBLANES, SUBLANES, tb), axis=0)
        c8 = lax.fori_loop(0, nkb, body, jnp.zeros((SUBLANES, tb), F32))
        return jnp.sum(c8, axis=0, keepdims=True)

    def threshold_of(n_blocks):
        def count16(ref, pred_fn):
            parts = []
            for j in range(n_blocks):
                ones = jnp.where(pred_fn(ref[j]), jnp.ones((tb, tb), I16), jnp.zeros((tb, tb), I16))
                parts += [ones[g * PACKED_ROWS:(g + 1) * PACKED_ROWS, :] for g in range(tb // PACKED_ROWS)]
            while len(parts) > 1:
                parts = [a + b for a, b in zip(parts[0::2], parts[1::2])] + ([parts[-1]] if len(parts) % 2 else [])
            return jnp.sum(parts[0].astype(F32), axis=0, keepdims=True)

        def search16(ref, base):
            def bit_body(t, tau):
                cand = tau + jnp.left_shift(jnp.int32(1), 15 - t)
                cand16 = cand.astype(I16)
                c = base + count16(ref, lambda blk: blk >= cand16)
                return jnp.where(c >= kf, cand, tau)
            return lax.fori_loop(0, 16, bit_body, jnp.full((1, tb), I16_MIN, I32))

        tau_hi = search16(sch, 0.0)
        tau_hi16 = tau_hi.astype(I16)
        above = count16(sch, lambda blk: blk > tau_hi16)
        for j in range(n_blocks):
            scl[j] = jnp.where(sch[j] == tau_hi16, scl[j], jnp.full((tb, tb), I16_MIN, I16))
        tau_lo = search16(scl, above)
        return tau_hi * 65536 + (tau_lo - I16_MIN)

    tau = lax.switch(i, [functools.partial(threshold_of, n) for n in range(off + 1, length // tb + 1)])
    cnt_ge = count(lambda blk, ks: blk >= tau)
    has_thr = tau > INT_MIN
    tie = has_thr & (cnt_ge > kf)
    any_tie = jnp.max(jnp.where(tie, 1.0, 0.0)) > 0.0

    def write_mask(sel_fn):
        def body(j, _):
            mk[j] = jnp.where(sel_fn(sc[j], j * tb), 0.0, NEG)
            return 0
        lax.fori_loop(0, nkb, body, 0)

    @pl.when(jnp.logical_not(any_tie))
    def _plain():
        thr = jnp.where(has_thr, tau, INT_MIN + 1)
        write_mask(lambda blk, ks: blk >= thr)

    @pl.when(any_tie)
    def _ties():
        need = kf - count(lambda blk, ks: blk > tau)
        q_idx = jnp.zeros((1, tb), I32)
        for bit in reversed(range(max(1, (length - 1).bit_length()))):
            cand = q_idx + (1 << bit)
            c = count(lambda blk, ks: (blk == tau) & ((key_l + ks) < cand))
            q_idx = jnp.where(c < need, cand, q_idx)
        last_eq = jnp.where(has_thr, jnp.where(tie, q_idx, length), -1)
        write_mask(lambda blk, ks: (blk > tau) | ((blk == tau) & ((key_l + ks) <= last_eq)))

    m_ref[...] = jnp.full(m_ref.shape, NEG, F32)
    l_ref[...] = jnp.zeros(l_ref.shape, F32)
    acc_ref[...] = jnp.zeros(acc_ref.shape, F32)
    row_lo = lax.broadcasted_iota(I32, (LANES, 1), 0) < HALF
    n_far = jnp.maximum(qblk - nd + 1, 0)

    def logits_rows(j0, ds):
        n_blk = len(ds)
        ks = pl.multiple_of(j0 * tb, tb)
        kblk = k2[pl.ds(ks, n_blk * tb), :]
        for hp in range(H_B // 2):
            for par, qm in enumerate(head_queries(qb_ref, hp)):
                h = 2 * hp + par
                rows = _nt_dot(kblk, qm)
                for b, d in enumerate(ds):
                    s = rows[b * tb:(b + 1) * tb, :] + mk[j0 + b]
                    if d is not None:
                        s = s + bt_ref[h, d]
                    s_ref[h, j0 + b] = s
                    top = jnp.max(s, axis=0, keepdims=True)
                    if d is None:
                        top = top + far_ref[h]
                    m_ref[h] = jnp.maximum(m_ref[h], top)

    def weigh_rows(j0, fars):
        ps = [[] for _ in fars]
        for h in range(H_B):
            for b, far in enumerate(fars):
                shift = m_ref[h] - far_ref[h] if far else m_ref[h]
                p = jnp.exp2(s_ref[h, j0 + b] - shift)
                l_ref[h] = l_ref[h] + jnp.sum(p, axis=0, keepdims=True)
                ps[b].append(p.astype(BF16))
        for hp in range(H_B // 2):
            pv = None
            for b in range(len(fars)):
                part = _dot(vt_lo[j0 + b], ps[b][2 * hp]) + _dot(vt_hi[j0 + b], ps[b][2 * hp + 1])
                pv = part if pv is None else pv + part
            acc_ref[hp] = acc_ref[hp] + pv

    assert nd == 2

    def far_logits_group(t, _):
        logits_rows(GROUP * t, (None,) * GROUP)
        return 0

    lax.fori_loop(0, n_far // GROUP, far_logits_group, 0)
    for rest in range(1, GROUP):
        @pl.when(n_far % GROUP == rest)
        def _far_logits_tail(rest=rest):
            logits_rows(n_far - rest, (None,) * rest)

    @pl.when(qblk >= 1)
    def _near_logits_pair():
        logits_rows(qblk - 1, (1, 0))

    @pl.when(qblk == 0)
    def _near_logits_single():
        logits_rows(qblk, (0,))

    def far_weigh_pair(t, _):
        weigh_rows(2 * t, (True, True))
        return 0

    lax.fori_loop(0, n_far // 2, far_weigh_pair, 0)

    @pl.when(n_far % 2 == 1)
    def _far_weigh_single():
        weigh_rows(n_far - 1, (True,))

    @pl.when(qblk >= 1)
    def _near_weigh_pair():
        weigh_rows(qblk - 1, (False, False))

    @pl.when(qblk == 0)
    def _near_weigh_single():
        weigh_rows(qblk, (False,))

    for hp in range(H_B // 2):
        denom = jnp.where(row_lo, l_ref[2 * hp], l_ref[2 * hp + 1])
        o_ref[0, :, hp * LANES:(hp + 1) * LANES] = (acc_ref[hp] / denom).T


def _sparse_attention(q_b, q_i, w_i, k2_new, v2_new, i2_new, k_past, v_past, i_past, bias_tiles, bias_far,
                      n_select):
    b, t, _ = q_b.shape
    has_past = k_past is not None
    p = k_past.shape[1] if has_past else 0
    tb = ATT_BLOCK
    assert t % tb == 0 and p % tb == 0 and tb % CHUNK == 0 and bias_tiles.shape[2] == tb
    length = p + t
    nd = bias_tiles.shape[1]
    nb = length // tb
    qspec = pl.BlockSpec((1, tb, W_B), lambda bi, i: (bi, i, 0))
    new = pl.BlockSpec((1, t, LANES), lambda bi, i: (bi, 0, 0))
    past = pl.BlockSpec((1, p, HD_B), lambda bi, i: (bi, 0, 0))
    ins = [q_b, q_i, w_i, k2_new, v2_new, i2_new] + ([k_past, v_past, i_past] if has_past else [])
    specs = ([qspec, qspec, pl.BlockSpec((1, tb, LANES), lambda bi, i: (bi, i, 0)), new, new, new]
             + ([past, past, past] if has_past else []))
    ins += [bias_tiles, bias_far]
    specs += [pl.BlockSpec(bias_tiles.shape, lambda bi, i: (0, 0, 0, 0)), pl.BlockSpec(memory_space=pltpu.SMEM)]
    return pl.pallas_call(
        functools.partial(_dsa_kernel, tb=tb, off=p // tb, past_len=p, has_past=has_past,
                          n_select=n_select, nd=nd, length=length),
        grid=(b, t // tb),
        in_specs=specs,
        out_specs=qspec,
        out_shape=jax.ShapeDtypeStruct((b, t, W_B), F32),
        scratch_shapes=[pltpu.VMEM((length, LANES), BF16),
                        pltpu.VMEM((nb, LANES, tb), BF16), pltpu.VMEM((nb, LANES, tb), BF16),
                        pltpu.VMEM((length, LANES), BF16),
                        pltpu.VMEM((nb, tb, tb), I32),
                        pltpu.VMEM((nb, tb, tb), I16), pltpu.VMEM((nb, tb, tb), I16),
                        pltpu.VMEM((nb, tb, tb), F32),
                        pltpu.VMEM((H_B, nb, tb, tb), F32),
                        pltpu.VMEM((H_B, 1, tb), F32), pltpu.VMEM((H_B, 1, tb), F32),
                        pltpu.VMEM((H_B // 2, LANES, tb), F32)],
        compiler_params=_params(("parallel", "arbitrary")),
        name=f"dsa_t{t}",
    )(*ins)


def _sigmoid(x):
    return 1.0 / (1.0 + jnp.exp(-x))


def _merge_kernel(x_ref, ya_ref, yb_ref, wg_ref, bg_ref, wpa_ref, wpb_ref, wo_ref, g_ref, b_ref, o_ref):
    x = x_ref[...]
    xb = x.astype(BF16)

    def seg(c0, n):
        return _dot(xb, wg_ref[:, c0:c0 + n]) + bg_ref[:, c0:c0 + n]

    g_a = seg(C_GA, W_A)
    y_a = (ya_ref[...] * (g_a * _sigmoid(g_a))).astype(BF16)
    branch_a = _sigmoid(seg(C_RA, D_MODEL)) * _dot(y_a, wpa_ref[...])
    g_b = seg(C_GB, W_B)
    y_b = (yb_ref[...] * (g_b * _sigmoid(g_b))).astype(BF16)
    branch_b = _sigmoid(seg(C_RB, D_MODEL)) * _dot(y_b, wpb_ref[...])
    merged = (branch_a + branch_b).astype(BF16)
    o_ref[...] = _ln(ALPHA * x + _dot(merged, wo_ref[...]), g_ref[...], b_ref[...])


def _merge(x2d, y_a, y_b, w_gate, b_gate, w_pa, w_pb, w_out, ln_g, ln_b):
    n = x2d.shape[0]
    tm = min(TOKEN_TILE, n)
    row = lambda width: pl.BlockSpec((tm, width), lambda i: (i, 0))
    full = lambda a: pl.BlockSpec(a.shape, lambda i: (0, 0))
    consts = [w_gate, b_gate, w_pa, w_pb, w_out, ln_g, ln_b]
    return pl.pallas_call(
        _merge_kernel,
        grid=(n // tm,),
        in_specs=[row(D_MODEL), row(W_A), row(W_B)] + [full(a) for a in consts],
        out_specs=row(D_MODEL),
        out_shape=jax.ShapeDtypeStruct((n, D_MODEL), F32),
        compiler_params=_params(("parallel",)),
        name=f"merge_n{n}",
    )(x2d, y_a, y_b, *consts)


def _pack_weights(w_in, b_in):
    offs = np.concatenate([[0], np.cumsum(SPLIT_SIZES)])
    names = ("q_a", "k_a", "v_a", "g_a", "q_b", "k_b", "v_b", "g_b", "q_i", "k_i", "w_i", "r_a", "r_b")
    w = {nm: w_in[:, :, offs[k]:offs[k + 1]] for k, nm in enumerate(names)}
    b = {nm: b_in[:, offs[k]:offs[k + 1]] for k, nm in enumerate(names)}
    pad = LANES - H_IDX

    def build(parts, last):
        order = (parts["q_a"] * (SB_SCALE * LOG2E), parts["k_a"], parts["v_a"], parts["q_b"] * (ATT_SCALE * LOG2E),
                 parts["q_i"],
                 parts["k_b"], parts["k_b"], parts["v_b"], parts["v_b"], parts["k_i"], parts["k_i"], last)
        return jnp.concatenate(order, axis=-1)

    w_att = build(w, jnp.pad(w["w_i"], ((0, 0), (0, 0), (0, pad)))).astype(BF16)
    b_att = build(b, jnp.pad(b["w_i"], ((0, 0), (0, pad))))[:, None, :]
    w_gate = jnp.concatenate([w["g_a"], w["g_b"], w["r_a"], w["r_b"]], axis=-1).astype(BF16)
    b_gate = jnp.concatenate([b["g_a"], b["g_b"], b["r_a"], b["r_b"]], axis=-1)[:, None, :]
    return w_att, b_att, w_gate, b_gate


def _trunk_layer(x2d, batch, seq, past, n_select, weights, bias_tiles, bias_far, ln_in=None):
    w_att, b_att, w_gate, b_gate, w_pa, w_pb, w_out, ln_g, ln_b = weights
    (q_a, q_b, q_i, kb2, vb2, ki2, w_i), new_rows, x2d = _project(x2d, w_att, b_att, ln_in)
    shape3 = lambda a: a.reshape(batch, seq, a.shape[-1])
    if past is None:
        pk_a = pv_a = pk_b = pv_b = pk_i = None
    else:
        pk_a, pv_a, pk_b, pv_b, pk_i = past
        pk_a = pk_a.reshape(batch, -1, W_A)
        pv_a = pv_a.reshape(batch, -1, W_A)
    y_a = _stick_breaking(shape3(q_a), shape3(new_rows[0]), shape3(new_rows[1]), pk_a, pv_a)
    y_b = _sparse_attention(shape3(q_b), shape3(q_i), shape3(w_i), shape3(kb2), shape3(vb2), shape3(ki2),
                            pk_b, pv_b, pk_i, bias_tiles, bias_far, n_select)
    x_next = _merge(x2d, y_a.reshape(-1, W_A), y_b.reshape(-1, W_B),
                    w_gate, b_gate, w_pa, w_pb, w_out, ln_g, ln_b)
    return x_next, new_rows


def kernel(x_prompt, x_sample, cache_sb_k, cache_sb_v, cache_dsa_k, cache_dsa_v, cache_idx_k,
           ln_in_g, ln_in_b, w_in, b_in, w_proj_a, w_proj_b, w_out, ln_g, ln_b, rel_bias):
    batch, seq, _ = x_prompt.shape
    dec_batch, dec_seq, _ = x_sample.shape
    past_len = cache_sb_k.shape[2]
    dec_pad = -(-dec_seq // ATT_BLOCK) * ATT_BLOCK
    n_sel_prompt = min(MAX_SELECT, seq // 4)
    n_sel_sample = min(MAX_SELECT, (past_len + dec_seq) // 4)
    w_att, b_att, w_gate, b_gate = _pack_weights(w_in, b_in)
    w_pa = w_proj_a.astype(BF16)
    w_pb = w_proj_b.astype(BF16)
    w_o = w_out.astype(BF16)
    bias_far = rel_bias[N_BUCKETS // 2 - 1] * LOG2E
    tiles = _bias_tiles(rel_bias, ATT_BLOCK)
    hp = x_prompt.reshape(-1, D_MODEL)
    hs = jnp.pad(x_sample, ((0, 0), (0, dec_pad - dec_seq), (0, 0))).reshape(-1, D_MODEL)
    rows_p, rows_s = [], []
    for layer in range(DEPTH):
        weights = (w_att[layer], b_att[layer], w_gate[layer], b_gate[layer], w_pa[layer], w_pb[layer], w_o[layer],
                   ln_g[layer].reshape(1, D_MODEL), ln_b[layer].reshape(1, D_MODEL))
        ln_in = (ln_in_g, ln_in_b) if layer == 0 else None
        hp, new_p = _trunk_layer(hp, batch, seq, None, n_sel_prompt, weights, tiles, bias_far, ln_in)
        past = (cache_sb_k[layer], cache_sb_v[layer], cache_dsa_k[layer], cache_dsa_v[layer], cache_idx_k[layer])
        hs, new_s = _trunk_layer(hs, dec_batch, dec_pad, past, n_sel_sample, weights, tiles, bias_far, ln_in)
        rows_p.append(new_p)
        rows_s.append(new_s)

    def shaped(stacked, b, t_pad, t):
        k_a, v_a, k_b, v_b, k_i = stacked
        heads = lambda a: a.reshape(DEPTH, b, t_pad, H_A, HD_A)[:, :, :t]
        flat = lambda a: a.reshape(DEPTH, b, t_pad, a.shape[-1])[:, :, :t]
        return heads(k_a), heads(v_a), flat(k_b), flat(v_b), flat(k_i)

    y_s = hs.reshape(dec_batch, dec_pad, D_MODEL)[:, :dec_seq]
    return ((hp.reshape(batch, seq, D_MODEL), y_s) + shaped(_collect(rows_p), batch, seq, seq)
            + shaped(_collect(rows_s), dec_batch, dec_pad, dec_seq))
```
